```python
import jax, jax.numpy as jnp
from jax import lax
import numpy as np

D_MODEL = 1024
BATCH = 8
SEQ = 2048
DEPTH = 2

N_A_LAYERS = DEPTH // 2
N_B_LAYERS = DEPTH - N_A_LAYERS
GLA_HEADS = 4
GLA_KEY_DIM = D_MODEL // 2
GLA_VAL_DIM = D_MODEL
GLA_HEAD_K = GLA_KEY_DIM // GLA_HEADS
GLA_HEAD_V = GLA_VAL_DIM // GLA_HEADS
GLA_GATE_RANK = 16
GLA_GATE_NORMALIZER = 16.0
GLA_CHUNK = 64
GLA_SPLITS = [GLA_KEY_DIM, 2 * GLA_KEY_DIM, 2 * GLA_KEY_DIM + GLA_VAL_DIM, 2 * GLA_KEY_DIM + 2 * GLA_VAL_DIM]
GLA_IN_DIM = 2 * GLA_KEY_DIM + 2 * GLA_VAL_DIM + GLA_GATE_RANK
SB_HEADS = 16
SB_HEAD_DIM = D_MODEL // SB_HEADS
SB_WIDTH = SB_HEADS * SB_HEAD_DIM
SB_QBLOCK = 128
RMS_EPS = 1e-6

kernel_name = 'yoco_gla_stickbreaking_adaln'


def rms_norm(x, gain):
    x32 = x.astype(jnp.float32)
    y = x32 * lax.rsqrt(jnp.mean(x32 * x32, axis=-1, keepdims=True) + RMS_EPS)
    return (y * gain.astype(jnp.float32)).astype(x.dtype)


def ada_modulate(c, w, b, n):
    cond = jax.nn.silu(c) @ w + b
    return jnp.split(cond[:, None, :], n, axis=-1)


def gla_mixer(h, w_in, w_gk2, b_gk, o_gain, w_out):
    bsz, seq, _ = h.shape
    n_chunks = seq // GLA_CHUNK
    f32 = jnp.float32
    proj = h @ w_in
    q, k, v, g, gk_lr = jnp.split(proj, GLA_SPLITS, axis=-1)
    gk = jax.nn.log_sigmoid((gk_lr @ w_gk2 + b_gk).astype(f32)) / GLA_GATE_NORMALIZER

    def chunked(t, dh):
        return t.astype(f32).reshape(bsz, n_chunks, GLA_CHUNK, GLA_HEADS, dh).transpose(0, 3, 1, 2, 4)

    q = chunked(q, GLA_HEAD_K) * (GLA_HEAD_K ** -0.5)
    k = chunked(k, GLA_HEAD_K)
    v = chunked(v, GLA_HEAD_V)
    b = jnp.cumsum(chunked(gk, GLA_HEAD_K), axis=3)
    b_last = b[:, :, :, -1:, :]
    q_e = q * jnp.exp(b)
    k_e = k * jnp.exp(-b)
    k_to_end = k * jnp.exp(b_last - b)
    causal = jnp.tril(jnp.ones((GLA_CHUNK, GLA_CHUNK), dtype=bool))
    scores = jnp.where(causal, jnp.einsum('bhnid,bhnjd->bhnij', q_e, k_e), 0.0)
    o_intra = jnp.einsum('bhnij,bhnjv->bhniv', scores, v)
    state_inc = jnp.einsum('bhncd,bhncv->bhndv', k_to_end, v)
    decay = jnp.exp(b_last[:, :, :, 0, :])

    def step(state, inp):
        dec, inc = inp
        return dec[..., None] * state + inc, state

    s0 = jnp.zeros((bsz, GLA_HEADS, GLA_HEAD_K, GLA_HEAD_V), f32)
    _, s_prev = lax.scan(step, s0, (jnp.moveaxis(decay, 2, 0), jnp.moveaxis(state_inc, 2, 0)))
    s_prev = jnp.moveaxis(s_prev, 0, 2)
    o = o_intra + jnp.einsum('bhncd,bhndv->bhncv', q_e, s_prev)
    o = o.transpose(0, 2, 3, 1, 4).reshape(bsz, seq, GLA_HEADS, GLA_HEAD_V)
    o = rms_norm(o, o_gain).reshape(bsz, seq, GLA_VAL_DIM) * jax.nn.silu(g.astype(f32))
    return o.astype(h.dtype) @ w_out


def stick_breaking_mixer(h, k, v, w_in, w_out):
    bsz, seq, _ = h.shape
    q, g = jnp.split(h @ w_in, 2, axis=-1)
    q = q.reshape(bsz, seq, SB_HEADS, SB_HEAD_DIM).transpose(0, 2, 1, 3)
    scale = SB_HEAD_DIM ** -0.5
    blocks = []
    for start in range(0, seq, SB_QBLOCK):
        end = start + SB_QBLOCK
        z = jnp.einsum('bhtd,bhsd->bhts', q[:, :, start:end], k[:, :, :end]).astype(jnp.float32) * scale
        t_idx = start + jnp.arange(SB_QBLOCK)[:, None]
        s_idx = jnp.arange(end)[None, :]
        strictly_before = s_idx < t_idx
        log_beta = jax.nn.log_sigmoid(z)
        log_one_minus = jnp.where(strictly_before, log_beta - z, 0.0)
        log_survive = lax.cumsum(log_one_minus, axis=3, reverse=True) - log_one_minus
        weights = jnp.where(strictly_before, jnp.exp(log_beta + log_survive), 0.0)
        blocks.append(jnp.einsum('bhts,bhsd->bhtd', weights.astype(v.dtype), v[:, :, :end]))
    o = jnp.concatenate(blocks, axis=2).transpose(0, 2, 1, 3).reshape(bsz, seq, SB_WIDTH)
    o = o * jax.nn.silu(g)
    return o @ w_out


def setup_inputs(seed: int = 0) -> dict:
    key = jax.random.key(seed)
    ks = jax.random.split(key, 20)
    f32 = jnp.float32

    def dense(k, shape, fan_in, mult=1.0):
        return jax.random.normal(k, shape, f32) * (mult * fan_in ** -0.5)

    def gain(k, shape):
        return 1.0 + 0.02 * jax.random.normal(k, shape, f32)

    def bias(k, shape):
        return 0.02 * jax.random.normal(k, shape, f32)

    D = D_MODEL
    return {
        'x': jax.random.normal(ks[0], (BATCH, SEQ, D), f32),
        'c': jax.random.normal(ks[1], (BATCH, D), f32),
        'norm_gain': gain(ks[2], (DEPTH, D)),
        'w_ada': dense(ks[3], (DEPTH, D, 3 * D), D, 0.2),
        'b_ada': bias(ks[4], (DEPTH, 3 * D)),
        'gla_w_in': dense(ks[5], (N_A_LAYERS, D, GLA_IN_DIM), D),
        'gla_w_gk2': dense(ks[6], (N_A_LAYERS, GLA_GATE_RANK, GLA_KEY_DIM), GLA_GATE_RANK),
        'gla_b_gk': bias(ks[7], (N_A_LAYERS, GLA_KEY_DIM)),
        'gla_o_gain': gain(ks[8], (N_A_LAYERS, GLA_HEAD_V)),
        'gla_w_out': dense(ks[9], (N_A_LAYERS, GLA_VAL_DIM, D), GLA_VAL_DIM),
        'kv_gain': gain(ks[10], (D,)),
        'kv_w_ada': dense(ks[11], (D, 2 * D), D, 0.2),
        'kv_b_ada': bias(ks[12], (2 * D,)),
        'w_kv': dense(ks[13], (D, 2 * SB_WIDTH), D),
        'sb_w_in': dense(ks[14], (N_B_LAYERS, D, 2 * SB_WIDTH), D),
        'sb_w_out': dense(ks[15], (N_B_LAYERS, SB_WIDTH, D), SB_WIDTH),
        'final_gain': gain(ks[16], (D,)),
    }


def reference(x, c, norm_gain, w_ada, b_ada, gla_w_in, gla_w_gk2, gla_b_gk, gla_o_gain, gla_w_out,
              kv_gain, kv_w_ada, kv_b_ada, w_kv, sb_w_in, sb_w_out, final_gain):
    bsz, seq, _ = x.shape
    for i in range(N_A_LAYERS):
        shift, scale, gate = ada_modulate(c, w_ada[i], b_ada[i], 3)
        h = rms_norm(x, norm_gain[i]) * (1.0 + scale) + shift
        x = x + gate * gla_mixer(h, gla_w_in[i], gla_w_gk2[i], gla_b_gk[i], gla_o_gain[i], gla_w_out[i])
    kv_shift, kv_scale = ada_modulate(c, kv_w_ada, kv_b_ada, 2)
    hk = rms_norm(x, kv_gain) * (1.0 + kv_scale) + kv_shift
    k_sh, v_sh = jnp.split(hk @ w_kv, 2, axis=-1)
    k_sh = k_sh.reshape(bsz, seq, SB_HEADS, SB_HEAD_DIM).transpose(0, 2, 1, 3)
    v_sh = v_sh.reshape(bsz, seq, SB_HEADS, SB_HEAD_DIM).transpose(0, 2, 1, 3)
    for j in range(N_B_LAYERS):
        l = N_A_LAYERS + j
        shift, scale, gate = ada_modulate(c, w_ada[l], b_ada[l], 3)
        h = rms_norm(x, norm_gain[l]) * (1.0 + scale) + shift
        x = x + gate * stick_breaking_mixer(h, k_sh, v_sh, sb_w_in[j], sb_w_out[j])
    return rms_norm(x, final_gain)
```

```python
import functools
import math

import jax
import jax.numpy as jnp
from jax import lax
from jax.experimental import pallas as pl
from jax.experimental.pallas import tpu as pltpu

F32 = jnp.float32
BF16 = jnp.bfloat16

RMS_EPS = 1e-6
GLA_HEADS = 4
GLA_CHUNK = 64
GLA_GATE_RANK = 16
GLA_GATE_NORMALIZER = 16.0
SB_HEADS = 16
LANES = 128
LOG2E = 1.4426950408889634
VMEM_LIMIT = 48 * 1024 * 1024


def _dot(a, b):
    return jnp.dot(a, b, preferred_element_type=F32)


def _dot_nt(a, b):
    return lax.dot_general(a, b, (((1,), (1,)), ((), ())), preferred_element_type=F32)


def _silu(x):
    return x * jax.nn.sigmoid(x)


def _params(n_axes):
    return pltpu.CompilerParams(dimension_semantics=("arbitrary",) * n_axes,
                                vmem_limit_bytes=VMEM_LIMIT)


def _ada_kernel(c_ref, w_ref, b_ref, o_ref):
    s = _silu(c_ref[...])
    o_ref[...] = jnp.dot(s, w_ref[...], preferred_element_type=F32,
                         precision=lax.Precision.HIGHEST) + b_ref[...]


def _ada(c, w, b, tn=512):
    n_l, d, n = w.shape
    bsz = c.shape[0]
    return pl.pallas_call(
        _ada_kernel,
        grid=(n_l, n // tn),
        in_specs=[pl.BlockSpec((bsz, d), lambda l, j: (0, 0)),
                  pl.BlockSpec((None, d, tn), lambda l, j: (l, 0, j)),
                  pl.BlockSpec((None, 1, tn), lambda l, j: (l, 0, j))],
        out_specs=pl.BlockSpec((None, bsz, tn), lambda l, j: (l, 0, j)),
        out_shape=jax.ShapeDtypeStruct((n_l, bsz, n), F32),
        compiler_params=_params(2),
        name="ada",
    )(c, w, b.reshape(n_l, 1, n))


def _normed(x):
    return x * lax.rsqrt(jnp.mean(x * x, axis=-1, keepdims=True) + RMS_EPS)


def _gla_proj_kernel(x_ref, cond_ref, gain_ref, w_ref, wlr_ref, wgk_ref, bgk_ref,
                     q_ref, k_ref, v_ref, g_ref, gk_ref, *, d, dk):
    shift = cond_ref[:, 0:d]
    scale = cond_ref[:, d:2 * d]
    h = (_normed(x_ref[...]) * gain_ref[...] * (1.0 + scale) + shift).astype(BF16)
    q_ref[...] = _dot(h, w_ref[:, 0:dk])
    k_ref[...] = _dot(h, w_ref[:, dk:2 * dk])
    v_ref[...] = _dot(h, w_ref[:, 2 * dk:2 * dk + d]).astype(BF16)
    g_ref[...] = _dot(h, w_ref[:, 2 * dk + d:2 * dk + 2 * d])
    lr = _dot(h, wlr_ref[...]).astype(BF16)
    pre = _dot(lr, wgk_ref[...]) + bgk_ref[...]
    gk_ref[...] = jax.nn.log_sigmoid(pre) * (1.0 / GLA_GATE_NORMALIZER)


def _gla_proj(x2, cond, gain, w_main, w_lr, w_gk2, b_gk, seq, tm=256):
    n, d = x2.shape
    dk = w_gk2.shape[1]
    tpb = seq // tm
    row = lambda i: (i, 0)
    fixed = lambda i: (0, 0)
    return pl.pallas_call(
        functools.partial(_gla_proj_kernel, d=d, dk=dk),
        grid=(n // tm,),
        in_specs=[pl.BlockSpec((tm, d), row),
                  pl.BlockSpec((None, 1, 3 * d), lambda i: (i // tpb, 0, 0)),
                  pl.BlockSpec((1, d), fixed),
                  pl.BlockSpec(w_main.shape, fixed),
                  pl.BlockSpec(w_lr.shape, fixed),
                  pl.BlockSpec(w_gk2.shape, fixed),
                  pl.BlockSpec((1, dk), fixed)],
        out_specs=[pl.BlockSpec((tm, dk), row), pl.BlockSpec((tm, dk), row),
                   pl.BlockSpec((tm, d), row), pl.BlockSpec((tm, d), row),
                   pl.BlockSpec((tm, dk), row)],
        out_shape=[jax.ShapeDtypeStruct((n, dk), F32), jax.ShapeDtypeStruct((n, dk), F32),
                   jax.ShapeDtypeStruct((n, d), BF16), jax.ShapeDtypeStruct((n, d), F32),
                   jax.ShapeDtypeStruct((n, dk), F32)],
        compiler_params=_params(1),
        name="gla_proj",
    )(x2, cond, gain, w_main, w_lr, w_gk2, b_gk)


def _gla_core_kernel(q_ref, k_ref, gk_ref, v_ref, o_ref, *, seq, hk, hv):
    c = GLA_CHUNK
    ri = lax.broadcasted_iota(jnp.int32, (c, c), 0)
    ci = lax.broadcasted_iota(jnp.int32, (c, c), 1)
    causal = ci <= ri
    tril = jnp.where(causal, 1.0, 0.0).astype(BF16)
    qscale = hk ** -0.5

    def body(n, st):
        r = pl.ds(pl.multiple_of(n * c, c), c)
        qc, kc, gc = q_ref[r, :], k_ref[r, :], gk_ref[r, :]
        vc = v_ref[r, :]
        g_hi = gc.astype(BF16)
        g_lo = (gc - g_hi.astype(F32)).astype(BF16)
        b = _dot(tril, g_hi) + _dot(tril, g_lo)
        b_last = b[c - 1:c, :]
        qe = (qc * qscale * jnp.exp(b)).astype(BF16)
        ke = (kc * jnp.exp(-b)).astype(BF16)
        kte = (kc * jnp.exp(b_last - b)).astype(BF16)
        s = jnp.where(causal, _dot_nt(qe, ke), 0.0).astype(BF16)
        o = _dot(s, vc) + _dot_nt(qe, st.astype(BF16))
        o_ref[r, :] = o
        return st * jnp.exp(b_last) + _dot(vc.T, kte)

    lax.fori_loop(0, seq // c, body, jnp.zeros((hv, hk), F32))


def _gla_core(q, k, gk, v, bsz, seq):
    hk = q.shape[-1] // GLA_HEADS
    hv = v.shape[-1] // GLA_HEADS
    q3, k3, gk3 = (t.reshape(bsz, seq, -1) for t in (q, k, gk))
    v3 = v.reshape(bsz, seq, -1)
    bh = lambda b, h: (b, 0, h)
    return pl.pallas_call(
        functools.partial(_gla_core_kernel, seq=seq, hk=hk, hv=hv),
        grid=(bsz, GLA_HEADS),
        in_specs=[pl.BlockSpec((None, seq, hk), bh), pl.BlockSpec((None, seq, hk), bh),
                  pl.BlockSpec((None, seq, hk), bh), pl.BlockSpec((None, seq, hv), bh)],
        out_specs=pl.BlockSpec((None, seq, hv), bh),
        out_shape=jax.ShapeDtypeStruct((bsz, seq, GLA_HEADS * hv), F32),
        compiler_params=_params(2),
        name="gla_core",
    )(q3, k3, gk3, v3)


def _gla_out_kernel(o_ref, g_ref, x_ref, cond_ref, ogain_ref, w_ref, x1_ref, *, d, hv):
    gate = cond_ref[:, 2 * d:3 * d]
    o = o_ref[...]
    parts = [_normed(o[:, h * hv:(h + 1) * hv]) for h in range(d // hv)]
    y = jnp.concatenate(parts, axis=-1) * ogain_ref[...] * _silu(g_ref[...])
    x1_ref[...] = x_ref[...] + gate * _dot(y.astype(BF16), w_ref[...])


def _gla_out(o2, g, x2, cond, ogain_row, w_out, seq, tm=256):
    n, d = x2.shape
    tpb = seq // tm
    row = lambda i: (i, 0)
    fixed = lambda i: (0, 0)
    return pl.pallas_call(
        functools.partial(_gla_out_kernel, d=d, hv=d // GLA_HEADS),
        grid=(n // tm,),
        in_specs=[pl.BlockSpec((tm, d), row), pl.BlockSpec((tm, d), row), pl.BlockSpec((tm, d), row),
                  pl.BlockSpec((None, 1, 3 * d), lambda i: (i // tpb, 0, 0)),
                  pl.BlockSpec((1, d), fixed), pl.BlockSpec((d, d), fixed)],
        out_specs=pl.BlockSpec((tm, d), row),
        out_shape=jax.ShapeDtypeStruct((n, d), F32),
        compiler_params=_params(1),
        name="gla_out",
    )(o2, g, x2, cond, ogain_row, w_out)


def _kvq_proj_kernel(x_ref, ckv_ref, c2_ref, kvgain_ref, gain_ref, wkv_ref, win_ref,
                     k_ref, v_ref, q_ref, g_ref, *, d, qscale):
    y = _normed(x_ref[...])
    hkv = (y * kvgain_ref[...] * (1.0 + ckv_ref[:, d:2 * d]) + ckv_ref[:, 0:d]).astype(BF16)
    h2 = (y * gain_ref[...] * (1.0 + c2_ref[:, d:2 * d]) + c2_ref[:, 0:d]).astype(BF16)
    k_ref[...] = _dot(hkv, wkv_ref[:, 0:d]).astype(BF16)
    v_ref[...] = _dot(hkv, wkv_ref[:, d:2 * d]).astype(BF16)
    q_ref[...] = (_dot(h2, win_ref[:, 0:d]) * qscale).astype(BF16)
    g_ref[...] = _dot(h2, win_ref[:, d:2 * d])


def _kvq_proj(x1, ckv, c2, kvgain, gain, w_kv, w_in, seq, qscale, tm=256):
    n, d = x1.shape
    tpb = seq // tm
    row = lambda i: (i, 0)
    fixed = lambda i: (0, 0)
    return pl.pallas_call(
        functools.partial(_kvq_proj_kernel, d=d, qscale=qscale),
        grid=(n // tm,),
        in_specs=[pl.BlockSpec((tm, d), row),
                  pl.BlockSpec((None, 1, 2 * d), lambda i: (i // tpb, 0, 0)),
                  pl.BlockSpec((None, 1, 3 * d), lambda i: (i // tpb, 0, 0)),
                  pl.BlockSpec((1, d), fixed), pl.BlockSpec((1, d), fixed),
                  pl.BlockSpec((d, 2 * d), fixed), pl.BlockSpec((d, 2 * d), fixed)],
        out_specs=[pl.BlockSpec((tm, d), row)] * 4,
        out_shape=[jax.ShapeDtypeStruct((n, d), BF16)] * 3 + [jax.ShapeDtypeStruct((n, d), F32)],
        compiler_params=_params(1),
        name="kvq_proj",
    )(x1, ckv, c2, kvgain, gain, w_kv, w_in)


def _split_bf16(x):
    hi = x.astype(BF16)
    return hi, (x - hi.astype(F32)).astype(BF16)


def _sb_attn_kernel(q_ref, k_ref, v_ref, o_ref, *, blk, dh):
    i = pl.program_id(2)
    lane = lax.broadcasted_iota(jnp.int32, (blk, 2 * dh), 1)
    q2 = q_ref[...]
    q_heads = (jnp.where(lane < dh, q2, jnp.zeros_like(q2)), jnp.where(lane >= dh, q2, jnp.zeros_like(q2)))
    ri = lax.broadcasted_iota(jnp.int32, (blk, blk), 0)
    ci = lax.broadcasted_iota(jnp.int32, (blk, blk), 1)
    strict = jnp.where(ci < ri, 1.0, 0.0).astype(BF16)
    below = ci < ri

    def block(qh, kb, vb, carry, acc, masked):
        z = _dot_nt(qh, kb)
        lg = jnp.log2(1.0 + jnp.exp2(-jnp.abs(z)))
        lb = jnp.minimum(z, 0.0) - lg
        lm = lb - z
        if masked:
            lm = jnp.where(below, lm, 0.0)
        hi, lo = _split_bf16(lm)
        carry_full = jnp.concatenate([carry] * (blk // LANES), axis=1)
        cexc = _dot(hi, strict) + _dot(lo, strict) + carry_full
        w = jnp.exp2(lb + cexc)
        if masked:
            w = jnp.where(below, w, 0.0)
        acc = acc + _dot(w.astype(BF16), vb)
        carry = jnp.broadcast_to(cexc[:, 0:1] + lm[:, 0:1], carry.shape)
        return carry, acc

    zeros = jnp.zeros((blk, 2 * dh), F32)
    r0 = pl.ds(pl.multiple_of(i * blk, blk), blk)
    kd, vd = k_ref[r0, :], v_ref[r0, :]
    state = []
    for qh in q_heads:
        state.extend(block(qh, kd, vd, zeros, zeros, True))

    def body(it, st):
        r = pl.ds(pl.multiple_of((i - 1 - it) * blk, blk), blk)
        kb, vb = k_ref[r, :], v_ref[r, :]
        ca, aa = block(q_heads[0], kb, vb, st[0], st[1], False)
        cb, ab = block(q_heads[1], kb, vb, st[2], st[3], False)
        return ca, aa, cb, ab

    st = lax.fori_loop(0, i, body, tuple(state))
    o_ref[...] = jnp.where(lane < dh, st[1], st[3])


def _sb_attn(q, k, v, bsz, seq, blk=256):
    d = q.shape[-1]
    dh = d // SB_HEADS
    q3, k3, v3 = (t.reshape(bsz, seq, d) for t in (q, k, v))
    return pl.pallas_call(
        functools.partial(_sb_attn_kernel, blk=blk, dh=dh),
        grid=(bsz, d // (2 * dh), seq // blk),
        in_specs=[pl.BlockSpec((None, blk, 2 * dh), lambda b, p, i: (b, i, p)),
                  pl.BlockSpec((None, seq, 2 * dh), lambda b, p, i: (b, 0, p)),
                  pl.BlockSpec((None, seq, 2 * dh), lambda b, p, i: (b, 0, p))],
        out_specs=pl.BlockSpec((None, blk, 2 * dh), lambda b, p, i: (b, i, p)),
        out_shape=jax.ShapeDtypeStruct((bsz, seq, d), F32),
        compiler_params=_params(3),
        name="sb_attn",
    )(q3, k3, v3)


def _sb_out_kernel(o_ref, g_ref, x_ref, cond_ref, fgain_ref, w_ref, out_ref, *, d):
    gate = cond_ref[:, 2 * d:3 * d]
    y = (o_ref[...] * _silu(g_ref[...])).astype(BF16)
    x2 = x_ref[...] + gate * _dot(y, w_ref[...])
    out_ref[...] = _normed(x2) * fgain_ref[...]


def _sb_out(o2, g, x1, cond, fgain, w_out, seq, tm=256):
    n, d = x1.shape
    tpb = seq // tm
    row = lambda i: (i, 0)
    fixed = lambda i: (0, 0)
    return pl.pallas_call(
        functools.partial(_sb_out_kernel, d=d),
        grid=(n // tm,),
        in_specs=[pl.BlockSpec((tm, d), row), pl.BlockSpec((tm, d), row), pl.BlockSpec((tm, d), row),
                  pl.BlockSpec((None, 1, 3 * d), lambda i: (i // tpb, 0, 0)),
                  pl.BlockSpec((1, d), fixed), pl.BlockSpec((d, d), fixed)],
        out_specs=pl.BlockSpec((tm, d), row),
        out_shape=jax.ShapeDtypeStruct((n, d), F32),
        compiler_params=_params(1),
        name="sb_out",
    )(o2, g, x1, cond, fgain, w_out)


def kernel(x, c, norm_gain, w_ada, b_ada, gla_w_in, gla_w_gk2, gla_b_gk, gla_o_gain, gla_w_out,
           kv_gain, kv_w_ada, kv_b_ada, w_kv, sb_w_in, sb_w_out, final_gain):
    bsz, seq, d = x.shape
    assert w_ada.shape[0] == 2 and gla_w_in.shape[0] == 1 and sb_w_in.shape[0] == 1
    dk = gla_w_gk2.shape[-1]
    n = bsz * seq
    x2 = x.reshape(n, d)

    cond = _ada(c, w_ada, b_ada)
    cond_kv = _ada(c, kv_w_ada[None], kv_b_ada[None])[0]
    cond1 = cond[0].reshape(bsz, 1, 3 * d)
    cond2 = cond[1].reshape(bsz, 1, 3 * d)
    cond_kv = cond_kv.reshape(bsz, 1, 2 * d)

    w_in = gla_w_in[0]
    n_main = 2 * dk + 2 * d
    w_main = w_in[:, :n_main].astype(BF16)
    w_lr = jnp.pad(w_in[:, n_main:], ((0, 0), (0, LANES - GLA_GATE_RANK))).astype(BF16)
    w_gk2 = jnp.pad(gla_w_gk2[0], ((0, LANES - GLA_GATE_RANK), (0, 0))).astype(BF16)
    q, k, v, g, gk = _gla_proj(x2, cond1, norm_gain[0:1], w_main, w_lr, w_gk2,
                               gla_b_gk[0:1], seq)
    o = _gla_core(q, k, gk, v, bsz, seq)
    ogain_row = jnp.tile(gla_o_gain[0], GLA_HEADS).reshape(1, d)
    x1 = _gla_out(o.reshape(n, d), g, x2, cond1, ogain_row, gla_w_out[0].astype(BF16), seq)

    dh = d // SB_HEADS
    qscale = (dh ** -0.5) * LOG2E
    ks, vs, qs, g2 = _kvq_proj(x1, cond_kv, cond2, kv_gain.reshape(1, d), norm_gain[1:2],
                               w_kv.astype(BF16), sb_w_in[0].astype(BF16), seq, qscale)
    o2 = _sb_attn(qs, ks, vs, bsz, seq)
    out = _sb_out(o2.reshape(n, d), g2, x1, cond2, final_gain.reshape(1, d),
                  sb_w_out[0].astype(BF16), seq)
    return out.reshape(bsz, seq, d)
```

```python
import functools

import jax
import jax.numpy as jnp
from jax import lax
from jax.experimental import pallas as pl
from jax.experimental.pallas import tpu as pltpu

F32 = jnp.float32
BF16 = jnp.bfloat16

RMS_EPS = 1e-6
GLA_HEADS = 4
GLA_CHUNK = 64
GLA_GATE_RANK = 16
GLA_GATE_NORMALIZER = 16.0
SB_HEADS = 16
LANES = 128
MXU_TILE = 256
LOG2E = 1.4426950408889634
VMEM_LIMIT = 48 * 1024 * 1024
SB_SKIP_LOG2 = -160.0


def _dot(a, b):
    return jnp.dot(a, b, preferred_element_type=F32)


def _dot_nt(a, b):
    return lax.dot_general(a, b, (((1,), (1,)), ((), ())), preferred_element_type=F32)


def _silu(x):
    return x * jax.nn.sigmoid(x)


def _normed(x):
    return x * lax.rsqrt(jnp.mean(x * x, axis=-1, keepdims=True) + RMS_EPS)


def _split_bf16(x):
    hi = x.astype(BF16)
    return hi, (x - hi.astype(F32)).astype(BF16)


def _params(n_axes):
    return pltpu.CompilerParams(dimension_semantics=("arbitrary",) * n_axes,
                                vmem_limit_bytes=VMEM_LIMIT)


def _ada_kernel(c_ref, w_ref, b_ref, o_ref):
    s = _silu(c_ref[...])
    o_ref[...] = jnp.dot(s, w_ref[...], preferred_element_type=F32,
                         precision=lax.Precision.HIGHEST) + b_ref[...]


def _ada(c, w, b, tn=512):
    n_l, d, n = w.shape
    bsz = c.shape[0]
    return pl.pallas_call(
        _ada_kernel,
        grid=(n_l, n // tn),
        in_specs=[pl.BlockSpec((bsz, d), lambda l, j: (0, 0)),
                  pl.BlockSpec((None, d, tn), lambda l, j: (l, 0, j)),
                  pl.BlockSpec((None, 1, tn), lambda l, j: (l, 0, j))],
        out_specs=pl.BlockSpec((None, bsz, tn), lambda l, j: (l, 0, j)),
        out_shape=jax.ShapeDtypeStruct((n_l, bsz, n), F32),
        compiler_params=_params(2),
        name="ada",
    )(c, w, b.reshape(n_l, 1, n))


def _gla_proj_kernel(x_ref, cond_ref, gain_ref, w_ref, wlr_ref, wgk_ref, bgk_ref,
                     qe_ref, ke_ref, v_ref, g_ref, dec_ref, *, d, dk, tm, qscale):
    c = GLA_CHUNK
    half = MXU_TILE
    shift = cond_ref[:, 0:d]
    scale = cond_ref[:, d:2 * d]
    h = (_normed(x_ref[...]) * gain_ref[...] * (1.0 + scale) + shift).astype(BF16)
    q = _dot(h, w_ref[:, 0:dk])
    k = _dot(h, w_ref[:, dk:2 * dk])
    v_ref[...] = _dot(h, w_ref[:, 2 * dk:2 * dk + d]).astype(BF16)
    g_ref[...] = _dot(h, w_ref[:, 2 * dk + d:2 * dk + 2 * d])
    lr = _dot(h, wlr_ref[...]).astype(BF16)
    pre = _dot(lr, wgk_ref[...]) + bgk_ref[...]
    gk = jax.nn.log_sigmoid(pre) * (1.0 / GLA_GATE_NORMALIZER)
    hi, lo = _split_bf16(gk)

    ri = lax.broadcasted_iota(jnp.int32, (half, half), 0)
    ci = lax.broadcasted_iota(jnp.int32, (half, half), 1)
    same_chunk = (ri // c) == (ci // c)
    tril = jnp.where(jnp.logical_and(same_chunk, ci <= ri), 1.0, 0.0).astype(BF16)
    tril2 = jnp.concatenate([tril, tril], axis=1)
    for s in range(tm // half):
        rows = slice(s * half, (s + 1) * half)
        b = _dot(tril2, jnp.concatenate([hi[rows], lo[rows]], axis=0))
        qe_ref[rows, :] = (q[rows] * qscale * jnp.exp(b)).astype(BF16)
        ke_ref[rows, :] = (k[rows] * jnp.exp(-b)).astype(BF16)

    n_c = tm // c
    cj = lax.broadcasted_iota(jnp.int32, (n_c, tm), 1) // c
    cr = lax.broadcasted_iota(jnp.int32, (n_c, tm), 0)
    sel = jnp.where(cj == cr, 1.0, 0.0).astype(BF16)
    b_last = _dot(jnp.concatenate([sel, sel], axis=1), jnp.concatenate([hi, lo], axis=0))
    dec_ref[...] = jnp.exp(b_last)


def _gla_proj(x2, cond, gain, w_main, w_lr, w_gk2, b_gk, seq, tm=512):
    n, d = x2.shape
    dk = w_gk2.shape[1]
    tpb = seq // tm
    n_c = tm // GLA_CHUNK
    row = lambda i: (i, 0)
    fixed = lambda i: (0, 0)
    qscale = (dk // GLA_HEADS) ** -0.5
    return pl.pallas_call(
        functools.partial(_gla_proj_kernel, d=d, dk=dk, tm=tm, qscale=qscale),
        grid=(n // tm,),
        in_specs=[pl.BlockSpec((tm, d), row),
                  pl.BlockSpec((None, 1, 3 * d), lambda i: (i // tpb, 0, 0)),
                  pl.BlockSpec((1, d), fixed),
                  pl.BlockSpec(w_main.shape, fixed),
                  pl.BlockSpec(w_lr.shape, fixed),
                  pl.BlockSpec(w_gk2.shape, fixed),
                  pl.BlockSpec((1, dk), fixed)],
        out_specs=[pl.BlockSpec((tm, dk), row), pl.BlockSpec((tm, dk), row),
                   pl.BlockSpec((tm, d), row), pl.BlockSpec((tm, d), row),
                   pl.BlockSpec((n_c, dk), row)],
        out_shape=[jax.ShapeDtypeStruct((n, dk), BF16), jax.ShapeDtypeStruct((n, dk), BF16),
                   jax.ShapeDtypeStruct((n, d), BF16), jax.ShapeDtypeStruct((n, d), F32),
                   jax.ShapeDtypeStruct((n // GLA_CHUNK, dk), F32)],
        compiler_params=_params(1),
        name="gla_proj",
    )(x2, cond, gain, w_main, w_lr, w_gk2, b_gk)


def _gla_core_kernel(qe_ref, ke_ref, v_ref, dec_ref, o_ref, *, seq, hk, hv, unroll):
    c = GLA_CHUNK
    ri = lax.broadcasted_iota(jnp.int32, (c, c), 0)
    ci = lax.broadcasted_iota(jnp.int32, (c, c), 1)
    causal = ci <= ri

    def body(it, s):
        dec_t = dec_ref[pl.ds(pl.multiple_of(it * unroll, unroll), unroll), :].T
        rows = [pl.ds(pl.multiple_of((it * unroll + u) * c, c), c) for u in range(unroll)]
        qs = [qe_ref[r, :] for r in rows]
        ks = [ke_ref[r, :] for r in rows]
        vs = [v_ref[r, :] for r in rows]
        scores = [_dot_nt(q, k) for q, k in zip(qs, ks)]
        incs = [_dot(k.T, v) for k, v in zip(ks, vs)]
        masked = [jnp.where(causal, sc, 0.0).astype(BF16) for sc in scores]
        for u in range(unroll):
            lhs = jnp.concatenate([qs[u], masked[u]], axis=1)
            rhs = jnp.concatenate([s.astype(BF16), vs[u]], axis=0)
            o_ref[rows[u], :] = _dot(lhs, rhs)
            s = (s + incs[u]) * dec_t[:, u:u + 1]
        return s

    lax.fori_loop(0, seq // (c * unroll), body, jnp.zeros((hk, hv), F32))


def _gla_core(qe, ke, v, dec, bsz, seq, unroll=8):
    hk = qe.shape[-1] // GLA_HEADS
    hv = v.shape[-1] // GLA_HEADS
    qe3, ke3, v3 = (t.reshape(bsz, seq, -1) for t in (qe, ke, v))
    dec3 = dec.reshape(bsz, seq // GLA_CHUNK, -1)
    bh = lambda b, h: (b, 0, h)
    return pl.pallas_call(
        functools.partial(_gla_core_kernel, seq=seq, hk=hk, hv=hv, unroll=unroll),
        grid=(bsz, GLA_HEADS),
        in_specs=[pl.BlockSpec((None, seq, hk), bh), pl.BlockSpec((None, seq, hk), bh),
                  pl.BlockSpec((None, seq, hv), bh), pl.BlockSpec((None, seq // GLA_CHUNK, hk), bh)],
        out_specs=pl.BlockSpec((None, seq, hv), bh),
        out_shape=jax.ShapeDtypeStruct((bsz, seq, GLA_HEADS * hv), F32),
        compiler_params=_params(2),
        name="gla_core",
    )(qe3, ke3, v3, dec3)


def _gla_out_kernel(o_ref, g_ref, x_ref, cond_ref, ogain_ref, w_ref, x1_ref, *, d, hv):
    gate = cond_ref[:, 2 * d:3 * d]
    o = o_ref[...]
    parts = [_normed(o[:, h * hv:(h + 1) * hv]) for h in range(d // hv)]
    y = jnp.concatenate(parts, axis=-1) * ogain_ref[...] * _silu(g_ref[...])
    x1_ref[...] = x_ref[...] + gate * _dot(y.astype(BF16), w_ref[...])


def _gla_out(o2, g, x2, cond, ogain_row, w_out, seq, tm=256):
    n, d = x2.shape
    tpb = seq // tm
    row = lambda i: (i, 0)
    fixed = lambda i: (0, 0)
    return pl.pallas_call(
        functools.partial(_gla_out_kernel, d=d, hv=d // GLA_HEADS),
        grid=(n // tm,),
        in_specs=[pl.BlockSpec((tm, d), row), pl.BlockSpec((tm, d), row), pl.BlockSpec((tm, d), row),
                  pl.BlockSpec((None, 1, 3 * d), lambda i: (i // tpb, 0, 0)),
                  pl.BlockSpec((1, d), fixed), pl.BlockSpec((d, d), fixed)],
        out_specs=pl.BlockSpec((tm, d), row),
        out_shape=jax.ShapeDtypeStruct((n, d), F32),
        compiler_params=_params(1),
        name="gla_out",
    )(o2, g, x2, cond, ogain_row, w_out)


def _kvq_proj_kernel(x_ref, ckv_ref, c2_ref, kvgain_ref, gain_ref, wkv_ref, win_ref,
                     k_ref, v_ref, q_ref, g_ref, *, d, qscale):
    y = _normed(x_ref[...])
    hkv = (y * kvgain_ref[...] * (1.0 + ckv_ref[:, d:2 * d]) + ckv_ref[:, 0:d]).astype(BF16)
    h2 = (y * gain_ref[...] * (1.0 + c2_ref[:, d:2 * d]) + c2_ref[:, 0:d]).astype(BF16)
    k_ref[...] = _dot(hkv, wkv_ref[:, 0:d]).astype(BF16)
    v_ref[...] = _dot(hkv, wkv_ref[:, d:2 * d]).astype(BF16)
    q_ref[...] = (_dot(h2, win_ref[:, 0:d]) * qscale).astype(BF16)
    g_ref[...] = _dot(h2, win_ref[:, d:2 * d])


def _kvq_proj(x1, ckv, c2, kvgain, gain, w_kv, w_in, seq, qscale, tm=256):
    n, d = x1.shape
    tpb = seq // tm
    row = lambda i: (i, 0)
    fixed = lambda i: (0, 0)
    return pl.pallas_call(
        functools.partial(_kvq_proj_kernel, d=d, qscale=qscale),
        grid=(n // tm,),
        in_specs=[pl.BlockSpec((tm, d), row),
                  pl.BlockSpec((None, 1, 2 * d), lambda i: (i // tpb, 0, 0)),
                  pl.BlockSpec((None, 1, 3 * d), lambda i: (i // tpb, 0, 0)),
                  pl.BlockSpec((1, d), fixed), pl.BlockSpec((1, d), fixed),
                  pl.BlockSpec((d, 2 * d), fixed), pl.BlockSpec((d, 2 * d), fixed)],
        out_specs=[pl.BlockSpec((tm, d), row)] * 4,
        out_shape=[jax.ShapeDtypeStruct((n, d), BF16)] * 3 + [jax.ShapeDtypeStruct((n, d), F32)],
        compiler_params=_params(1),
        name="kvq_proj",
    )(x1, ckv, c2, kvgain, gain, w_kv, w_in)


def _sb_attn_kernel(q_ref, k_ref, v_ref, o_ref, *, blk, dh):
    i = pl.program_id(2)
    lane = lax.broadcasted_iota(jnp.int32, (blk, 2 * dh), 1)
    q2 = q_ref[...]
    zero = jnp.zeros_like(q2)
    q_all = jnp.concatenate([jnp.where(lane < dh, q2, zero), jnp.where(lane >= dh, q2, zero)], axis=0)
    ri = lax.broadcasted_iota(jnp.int32, (blk, blk), 0)
    ci = lax.broadcasted_iota(jnp.int32, (blk, blk), 1)
    strict = jnp.where(ci < ri, 1.0, 0.0).astype(BF16)
    strict2 = jnp.concatenate([strict, strict], axis=0)
    ri2 = lax.broadcasted_iota(jnp.int32, (2 * blk, blk), 0)
    ci2 = lax.broadcasted_iota(jnp.int32, (2 * blk, blk), 1)
    below = ci2 < (ri2 & (blk - 1))

    def rows(j):
        return pl.ds(pl.multiple_of(j * blk, blk), blk)

    def run_blocks(js, diag_first, carry):
        kbs = [k_ref[rows(j), :] for j in js]
        vbs = [v_ref[rows(j), :] for j in js]
        zs = [_dot_nt(q_all, kb) for kb in kbs]
        lbs, lms, cats = [], [], []
        for n, z in enumerate(zs):
            lg = jnp.log2(1.0 + jnp.exp2(-jnp.abs(z)))
            lb = jnp.minimum(z, 0.0) - lg
            lm = lb - z
            if diag_first and n == 0:
                lm = jnp.where(below, lm, 0.0)
            hi, lo = _split_bf16(lm)
            lbs.append(lb)
            lms.append(lm)
            cats.append(jnp.concatenate([hi, lo], axis=1))
        sums = [_dot(cat, strict2) for cat in cats]
        ws = []
        for n in range(len(js)):
            cexc = sums[n] + jnp.concatenate([carry] * (blk // LANES), axis=1)
            w = jnp.exp2(lbs[n] + cexc)
            if diag_first and n == 0:
                w = jnp.where(below, w, 0.0)
            ws.append(w.astype(BF16))
            carry = jnp.broadcast_to(cexc[:, 0:1] + lms[n][:, 0:1], carry.shape)
        acc = _dot(ws[0], vbs[0])
        for w, vb in zip(ws[1:], vbs[1:]):
            acc = acc + _dot(w, vb)
        return acc, carry

    def finish(acc):
        o_ref[...] = jnp.where(lane < dh, acc[:blk], acc[blk:])

    carry0 = jnp.zeros((2 * blk, LANES), F32)

    @pl.when(i == 0)
    def _():
        acc, _ = run_blocks([i], True, carry0)
        finish(acc)

    @pl.when(i > 0)
    def _():
        acc, carry = run_blocks([i, i - 1], True, carry0)

        def cond(st):
            j, cmax, _, _ = st
            return jnp.logical_and(j >= 0, cmax >= SB_SKIP_LOG2)

        def body(st):
            j, _, carry, acc = st
            pv, carry = run_blocks([j], False, carry)
            return j - 1, jnp.max(carry), carry, acc + pv

        st = lax.while_loop(cond, body, (i - 2, jnp.max(carry), carry, acc))
        finish(st[3])


def _sb_attn(q, k, v, bsz, seq, blk=256):
    d = q.shape[-1]
    dh = d // SB_HEADS
    q3, k3, v3 = (t.reshape(bsz, seq, d) for t in (q, k, v))
    return pl.pallas_call(
        functools.partial(_sb_attn_kernel, blk=blk, dh=dh),
        grid=(bsz, d // (2 * dh), seq // blk),
        in_specs=[pl.BlockSpec((None, blk, 2 * dh), lambda b, p, i: (b, i, p)),
                  pl.BlockSpec((None, seq, 2 * dh), lambda b, p, i: (b, 0, p)),
                  pl.BlockSpec((None, seq, 2 * dh), lambda b, p, i: (b, 0, p))],
        out_specs=pl.BlockSpec((None, blk, 2 * dh), lambda b, p, i: (b, i, p)),
        out_shape=jax.ShapeDtypeStruct((bsz, seq, d), F32),
        compiler_params=_params(3),
        name="sb_attn",
    )(q3, k3, v3)


def _sb_out_kernel(o_ref, g_ref, x_ref, cond_ref, fgain_ref, w_ref, out_ref, *, d):
    gate = cond_ref[:, 2 * d:3 * d]
    y = (o_ref[...] * _silu(g_ref[...])).astype(BF16)
    x2 = x_ref[...] + gate * _dot(y, w_ref[...])
    out_ref[...] = _normed(x2) * fgain_ref[...]


def _sb_out(o2, g, x1, cond, fgain, w_out, seq, tm=256):
    n, d = x1.shape
    tpb = seq // tm
    row = lambda i: (i, 0)
    fixed = lambda i: (0, 0)
    return pl.pallas_call(
        functools.partial(_sb_out_kernel, d=d),
        grid=(n // tm,),
        in_specs=[pl.BlockSpec((tm, d), row), pl.BlockSpec((tm, d), row), pl.BlockSpec((tm, d), row),
                  pl.BlockSpec((None, 1, 3 * d), lambda i: (i // tpb, 0, 0)),
                  pl.BlockSpec((1, d), fixed), pl.BlockSpec((d, d), fixed)],
        out_specs=pl.BlockSpec((tm, d), row),
        out_shape=jax.ShapeDtypeStruct((n, d), F32),
        compiler_params=_params(1),
        name="sb_out",
    )(o2, g, x1, cond, fgain, w_out)


def kernel(x, c, norm_gain, w_ada, b_ada, gla_w_in, gla_w_gk2, gla_b_gk, gla_o_gain, gla_w_out,
           kv_gain, kv_w_ada, kv_b_ada, w_kv, sb_w_in, sb_w_out, final_gain):
    bsz, seq, d = x.shape
    assert w_ada.shape[0] == 2 and gla_w_in.shape[0] == 1 and sb_w_in.shape[0] == 1
    dk = gla_w_gk2.shape[-1]
    n = bsz * seq
    x2 = x.reshape(n, d)

    cond = _ada(c, w_ada, b_ada)
    cond_kv = _ada(c, kv_w_ada[None], kv_b_ada[None])[0]
    cond1 = cond[0].reshape(bsz, 1, 3 * d)
    cond2 = cond[1].reshape(bsz, 1, 3 * d)
    cond_kv = cond_kv.reshape(bsz, 1, 2 * d)

    w_in = gla_w_in[0]
    n_main = 2 * dk + 2 * d
    w_main = w_in[:, :n_main].astype(BF16)
    w_lr = jnp.pad(w_in[:, n_main:], ((0, 0), (0, LANES - GLA_GATE_RANK))).astype(BF16)
    w_gk2 = jnp.pad(gla_w_gk2[0], ((0, LANES - GLA_GATE_RANK), (0, 0))).astype(BF16)
    qe, ke, v, g, dec = _gla_proj(x2, cond1, norm_gain[0:1], w_main, w_lr, w_gk2,
                                  gla_b_gk[0:1], seq)
    o = _gla_core(qe, ke, v, dec, bsz, seq)
    ogain_row = jnp.tile(gla_o_gain[0], GLA_HEADS).reshape(1, d)
    x1 = _gla_out(o.reshape(n, d), g, x2, cond1, ogain_row, gla_w_out[0].astype(BF16), seq)

    dh = d // SB_HEADS
    qscale = (dh ** -0.5) * LOG2E
    ks, vs, qs, g2 = _kvq_proj(x1, cond_kv, cond2, kv_gain.reshape(1, d), norm_gain[1:2],
                               w_kv.astype(BF16), sb_w_in[0].astype(BF16), seq, qscale)
    o2 = _sb_attn(qs, ks, vs, bsz, seq)
    out = _sb_out(o2.reshape(n, d), g2, x1, cond2, final_gain.reshape(1, d),
                  sb_w_out[0].astype(BF16), seq)
    return out.reshape(bsz, seq, d)
```

```python
import functools

import jax
import jax.numpy as jnp
from jax import lax
from jax.experimental import pallas as pl
from jax.experimental.pallas import tpu as pltpu

F32 = jnp.float32
BF16 = jnp.bfloat16

RMS_EPS = 1e-6
GLA_HEADS = 4
GLA_CHUNK = 64
GLA_GATE_RANK = 16
GLA_GATE_NORMALIZER = 16.0
SB_HEADS = 16
LANES = 128
MXU_TILE = 256
LOG2E = 1.4426950408889634
VMEM_LIMIT = 48 * 1024 * 1024
SB_SKIP_LOG2 = -160.0


def _dot(a, b):
    return jnp.dot(a, b, preferred_element_type=F32)


def _dot_nt(a, b):
    return lax.dot_general(a, b, (((1,), (1,)), ((), ())), preferred_element_type=F32)


def _silu(x):
    return x * jax.nn.sigmoid(x)


def _normed(x):
    return x * lax.rsqrt(jnp.mean(x * x, axis=-1, keepdims=True) + RMS_EPS)


def _split_bf16(x):
    hi = x.astype(BF16)
    return hi, (x - hi.astype(F32)).astype(BF16)


def _params(n_axes):
    return pltpu.CompilerParams(dimension_semantics=("arbitrary",) * n_axes,
                                vmem_limit_bytes=VMEM_LIMIT)


def _ada_kernel(c_ref, w_ref, b_ref, o_ref):
    hi, lo = _split_bf16(_silu(c_ref[...]))
    w = w_ref[...].astype(BF16)
    o_ref[...] = _dot(hi, w) + _dot(lo, w) + b_ref[...]


def _ada(c, w, b, tn=512):
    n_l, d, n = w.shape
    bsz = c.shape[0]
    return pl.pallas_call(
        _ada_kernel,
        grid=(n_l, n // tn),
        in_specs=[pl.BlockSpec((bsz, d), lambda l, j: (0, 0)),
                  pl.BlockSpec((None, d, tn), lambda l, j: (l, 0, j)),
                  pl.BlockSpec((None, 1, tn), lambda l, j: (l, 0, j))],
        out_specs=pl.BlockSpec((None, bsz, tn), lambda l, j: (l, 0, j)),
        out_shape=jax.ShapeDtypeStruct((n_l, bsz, n), F32),
        compiler_params=_params(2),
        name="ada",
    )(c, w, b.reshape(n_l, 1, n))


def _gla_proj_kernel(x_ref, cond_ref, gain_ref, w_ref, wlr_ref, wgk_ref, bgk_ref,
                     qe_ref, ke_ref, v_ref, g_ref, dec_ref, *, d, dk, tm, qscale):
    c = GLA_CHUNK
    half = MXU_TILE
    shift = cond_ref[:, 0:d]
    scale = cond_ref[:, d:2 * d]
    h = (_normed(x_ref[...]) * gain_ref[...] * (1.0 + scale) + shift).astype(BF16)
    q = _dot(h, w_ref[:, 0:dk])
    k = _dot(h, w_ref[:, dk:2 * dk])
    v_ref[...] = _dot(h, w_ref[:, 2 * dk:2 * dk + d]).astype(BF16)
    g_ref[...] = _dot(h, w_ref[:, 2 * dk + d:2 * dk + 2 * d]).astype(BF16)
    lr = _dot(h, wlr_ref[...]).astype(BF16)
    pre = _dot(lr, wgk_ref[...]) + bgk_ref[...]
    gk = jax.nn.log_sigmoid(pre) * (1.0 / GLA_GATE_NORMALIZER)
    hi, lo = _split_bf16(gk)

    ri = lax.broadcasted_iota(jnp.int32, (half, half), 0)
    ci = lax.broadcasted_iota(jnp.int32, (half, half), 1)
    same_chunk = (ri // c) == (ci // c)
    tril = jnp.where(jnp.logical_and(same_chunk, ci <= ri), 1.0, 0.0).astype(BF16)
    tril2 = jnp.concatenate([tril, tril], axis=1)
    for s in range(tm // half):
        rows = slice(s * half, (s + 1) * half)
        b = _dot(tril2, jnp.concatenate([hi[rows], lo[rows]], axis=0))
        qe_ref[rows, :] = (q[rows] * qscale * jnp.exp(b)).astype(BF16)
        ke_ref[rows, :] = (k[rows] * jnp.exp(-b)).astype(BF16)

    n_c = tm // c
    cj = lax.broadcasted_iota(jnp.int32, (n_c, tm), 1) // c
    cr = lax.broadcasted_iota(jnp.int32, (n_c, tm), 0)
    sel = jnp.where(cj == cr, 1.0, 0.0).astype(BF16)
    b_last = _dot(jnp.concatenate([sel, sel], axis=1), jnp.concatenate([hi, lo], axis=0))
    dec_ref[...] = jnp.exp(b_last)


def _gla_proj(x2, cond, gain, w_main, w_lr, w_gk2, b_gk, seq, tm=512):
    n, d = x2.shape
    dk = w_gk2.shape[1]
    tpb = seq // tm
    n_c = tm // GLA_CHUNK
    row = lambda i: (i, 0)
    fixed = lambda i: (0, 0)
    qscale = (dk // GLA_HEADS) ** -0.5
    return pl.pallas_call(
        functools.partial(_gla_proj_kernel, d=d, dk=dk, tm=tm, qscale=qscale),
        grid=(n // tm,),
        in_specs=[pl.BlockSpec((tm, d), row),
                  pl.BlockSpec((None, 1, 3 * d), lambda i: (i // tpb, 0, 0)),
                  pl.BlockSpec((1, d), fixed),
                  pl.BlockSpec(w_main.shape, fixed),
                  pl.BlockSpec(w_lr.shape, fixed),
                  pl.BlockSpec(w_gk2.shape, fixed),
                  pl.BlockSpec((1, dk), fixed)],
        out_specs=[pl.BlockSpec((tm, dk), row), pl.BlockSpec((tm, dk), row),
                   pl.BlockSpec((tm, d), row), pl.BlockSpec((tm, d), row),
                   pl.BlockSpec((n_c, dk), row)],
        out_shape=[jax.ShapeDtypeStruct((n, dk), BF16), jax.ShapeDtypeStruct((n, dk), BF16),
                   jax.ShapeDtypeStruct((n, d), BF16), jax.ShapeDtypeStruct((n, d), BF16),
                   jax.ShapeDtypeStruct((n // GLA_CHUNK, dk), F32)],
        compiler_params=_params(1),
        name="gla_proj",
    )(x2, cond, gain, w_main, w_lr, w_gk2, b_gk)


def _gla_core_kernel(qe_ref, ke_ref, v_ref, dec_ref, g_ref, x_ref, cond_ref, ogain_ref, w_ref,
                     x1_ref, s_ref, *, tile, tiles_per_seq, d, hk, hv):
    c = GLA_CHUNK
    n_c = tile // c
    heads = range(GLA_HEADS)
    ri = lax.broadcasted_iota(jnp.int32, (c, c), 0)
    ci = lax.broadcasted_iota(jnp.int32, (c, c), 1)
    causal = ci <= ri

    @pl.when(pl.program_id(0) % tiles_per_seq == 0)
    def _():
        s_ref[...] = jnp.zeros(s_ref.shape, F32)

    def chunk(ref, h, u, width):
        return ref[u * c:(u + 1) * c, h * width:(h + 1) * width]

    work = [(h, u) for u in range(n_c) for h in heads]
    qs = {hu: chunk(qe_ref, *hu, hk) for hu in work}
    ks = {hu: chunk(ke_ref, *hu, hk) for hu in work}
    vs = {hu: chunk(v_ref, *hu, hv) for hu in work}
    scores = {hu: _dot_nt(qs[hu], ks[hu]) for hu in work}
    incs = {hu: _dot(ks[hu].T, vs[hu]) for hu in work}
    masked = {hu: jnp.where(causal, scores[hu], 0.0).astype(BF16) for hu in work}
    dec_t = [dec_ref[:, h * hk:(h + 1) * hk].T for h in heads]
    s = [s_ref[h] for h in heads]
    o = {}
    for h, u in work:
        lhs = jnp.concatenate([qs[h, u], masked[h, u]], axis=1)
        rhs = jnp.concatenate([s[h].astype(BF16), vs[h, u]], axis=0)
        o[h, u] = _dot(lhs, rhs)
        s[h] = (s[h] + incs[h, u]) * dec_t[h][:, u:u + 1]
    for h in heads:
        s_ref[h] = s[h]

    gate = cond_ref[:, 2 * d:3 * d]
    for u in range(n_c):
        rows = slice(u * c, (u + 1) * c)
        y = jnp.concatenate([_normed(o[h, u]) * ogain_ref[...] for h in heads], axis=1)
        y = (y * _silu(g_ref[rows, :].astype(F32))).astype(BF16)
        x1_ref[rows, :] = x_ref[rows, :] + gate * _dot(y, w_ref[...])


def _gla_core(qe, ke, v, dec, g, x2, cond, ogain, w_out, bsz, seq, tile=512):
    d = x2.shape[-1]
    dk = qe.shape[-1]
    n_c = tile // GLA_CHUNK
    tpb = seq // tile
    row = lambda i: (i, 0)
    fixed = lambda i: (0, 0)
    return pl.pallas_call(
        functools.partial(_gla_core_kernel, tile=tile, tiles_per_seq=tpb, d=d, hk=dk // GLA_HEADS,
                          hv=d // GLA_HEADS),
        grid=(bsz * tpb,),
        in_specs=[pl.BlockSpec((tile, dk), row), pl.BlockSpec((tile, dk), row),
                  pl.BlockSpec((tile, d), row), pl.BlockSpec((n_c, dk), row),
                  pl.BlockSpec((tile, d), row), pl.BlockSpec((tile, d), row),
                  pl.BlockSpec((None, 1, 3 * d), lambda i: (i // tpb, 0, 0)),
                  pl.BlockSpec((1, d // GLA_HEADS), fixed),
                  pl.BlockSpec((d, d), fixed)],
        out_specs=pl.BlockSpec((tile, d), row),
        out_shape=jax.ShapeDtypeStruct((bsz * seq, d), F32),
        scratch_shapes=[pltpu.VMEM((GLA_HEADS, dk // GLA_HEADS, d // GLA_HEADS), F32)],
        compiler_params=_params(1),
        name="gla_core",
    )(qe, ke, v, dec, g, x2, cond, ogain, w_out)


def _kvq_proj_kernel(x_ref, ckv_ref, c2_ref, kvgain_ref, gain_ref, wkv_ref, win_ref,
                     k_ref, v_ref, q_ref, g_ref, *, d, qscale):
    y = _normed(x_ref[...])
    hkv = (y * kvgain_ref[...] * (1.0 + ckv_ref[:, d:2 * d]) + ckv_ref[:, 0:d]).astype(BF16)
    h2 = (y * gain_ref[...] * (1.0 + c2_ref[:, d:2 * d]) + c2_ref[:, 0:d]).astype(BF16)
    k_ref[...] = _dot(hkv, wkv_ref[:, 0:d]).astype(BF16)
    v_ref[...] = _dot(hkv, wkv_ref[:, d:2 * d]).astype(BF16)
    q_ref[...] = (_dot(h2, win_ref[:, 0:d]) * qscale).astype(BF16)
    g_ref[...] = _dot(h2, win_ref[:, d:2 * d]).astype(BF16)


def _kvq_proj(x1, ckv, c2, kvgain, gain, w_kv, w_in, seq, qscale, tm=512):
    n, d = x1.shape
    tpb = seq // tm
    row = lambda i: (i, 0)
    fixed = lambda i: (0, 0)
    return pl.pallas_call(
        functools.partial(_kvq_proj_kernel, d=d, qscale=qscale),
        grid=(n // tm,),
        in_specs=[pl.BlockSpec((tm, d), row),
                  pl.BlockSpec((None, 1, 2 * d), lambda i: (i // tpb, 0, 0)),
                  pl.BlockSpec((None, 1, 3 * d), lambda i: (i // tpb, 0, 0)),
                  pl.BlockSpec((1, d), fixed), pl.BlockSpec((1, d), fixed),
                  pl.BlockSpec((d, 2 * d), fixed), pl.BlockSpec((d, 2 * d), fixed)],
        out_specs=[pl.BlockSpec((tm, d), row)] * 4,
        out_shape=[jax.ShapeDtypeStruct((n, d), BF16)] * 4,
        compiler_params=_params(1),
        name="kvq_proj",
    )(x1, ckv, c2, kvgain, gain, w_kv, w_in)


def _sb_attn_kernel(q_ref, k_ref, v_ref, o_ref, *, blk, dh, pairs):
    i = pl.program_id(2)
    pw = 2 * dh
    lane = lax.broadcasted_iota(jnp.int32, (blk, pw), 1)
    ri = lax.broadcasted_iota(jnp.int32, (blk, blk), 0)
    ci = lax.broadcasted_iota(jnp.int32, (blk, blk), 1)
    strict = jnp.where(ci < ri, 1.0, 0.0).astype(BF16)
    ri2 =lax.broadcasted_iota(jnp.int32, (2 * blk, blk), 0)
    ci2 = lax.broadcasted_iota(jnp.int32, (2 * blk, blk), 1)
    below = ci2 < (ri2 & (blk - 1))

    def rows(j):
        return pl.ds(pl.multiple_of(j * blk, blk), blk)

    def stacked_q(p):
        q2 = q_ref[:, p * pw:(p + 1) * pw]
        zero = jnp.zeros_like(q2)
        return jnp.concatenate([jnp.where(lane < dh, q2, zero), jnp.where(lane >= dh, q2, zero)], axis=0)

    def run_blocks(probs, js, diag_first, carries):
        work = [(p, n) for n in range(len(js)) for p in range(len(probs))]
        zs = {(p, n): _dot_nt(probs[p][1], k_ref[rows(js[n]), probs[p][0] * pw:(probs[p][0] + 1) * pw])
              for p, n in work}
        lbs, lms, cats = {}, {}, {}
        for key in work:
            z = zs[key]
            lg = jnp.log2(1.0 + jnp.exp2(-jnp.abs(z)))
            lb = jnp.minimum(z, 0.0) - lg
            lm = lb - z
            if diag_first and key[1] == 0:
                lm = jnp.where(below, lm, 0.0)
            lbs[key], lms[key], cats[key] = lb, lm, lm.astype(BF16)
        sums = {key: _dot(cats[key], strict) for key in work}
        carries = list(carries)
        ws = {}
        for p, n in work:
            carry = carries[p]
            cexc = sums[p, n] + jnp.concatenate([carry] * (blk // LANES), axis=1)
            w = jnp.exp2(lbs[p, n] + cexc)
            if diag_first and n == 0:
                w = jnp.where(below, w, 0.0)
            ws[p, n] = w.astype(BF16)
            carries[p] = jnp.broadcast_to(cexc[:, 0:1] + lms[p, n][:, 0:1], carry.shape)
        accs = []
        for p in range(len(probs)):
            lanes = slice(probs[p][0] * pw, (probs[p][0] + 1) * pw)
            acc = _dot(ws[p, 0], v_ref[rows(js[0]), lanes])
            for n in range(1, len(js)):
                acc = acc + _dot(ws[p, n], v_ref[rows(js[n]), lanes])
            accs.append(acc)
        return accs, carries

    def finish(p, acc):
        o_ref[:, p * pw:(p + 1) * pw] = jnp.where(lane < dh, acc[:blk], acc[blk:]).astype(o_ref.dtype)

    carry0 = jnp.zeros((2 * blk, LANES), F32)
    probs = [(p, stacked_q(p)) for p in range(pairs)]

    @pl.when(i == 0)
    def _():
        accs, _ = run_blocks(probs, [i], True, [carry0] * pairs)
        for p in range(pairs):
            finish(p, accs[p])

    @pl.when(i > 0)
    def _():
        accs, carries = run_blocks(probs, [i, i - 1], True, [carry0] * pairs)
        for p in range(pairs):
            def cond(st):
                j, cmax, _, _ = st
                return jnp.logical_and(j >= 0, cmax >= SB_SKIP_LOG2)

            def body(st, p=p):
                j, _, carry, acc = st
                pv, cs = run_blocks([probs[p]], [j], False, [carry])
                return j - 1, jnp.max(cs[0]), cs[0], acc + pv[0]

            st = lax.while_loop(cond, body, (i - 2, jnp.max(carries[p]), carries[p], accs[p]))
            finish(p, st[3])


def _sb_attn(q, k, v, bsz, seq, blk=256, pairs=2):
    d = q.shape[-1]
    dh = d // SB_HEADS
    gw = pairs * 2 * dh
    q3, k3, v3 = (t.reshape(bsz, seq, d) for t in (q, k, v))
    return pl.pallas_call(
        functools.partial(_sb_attn_kernel, blk=blk, dh=dh, pairs=pairs),
        grid=(bsz, d // gw, seq // blk),
        in_specs=[pl.BlockSpec((None, blk, gw), lambda b, p, i: (b, i, p)),
                  pl.BlockSpec((None, seq, gw), lambda b, p, i: (b, 0, p)),
                  pl.BlockSpec((None, seq, gw), lambda b, p, i: (b, 0, p))],
        out_specs=pl.BlockSpec((None, blk, gw), lambda b, p, i: (b, i, p)),
        out_shape=jax.ShapeDtypeStruct((bsz, seq, d), BF16),
        compiler_params=_params(3),
        name="sb_attn",
    )(q3, k3, v3)


def _sb_out_kernel(o_ref, g_ref, x_ref, cond_ref, fgain_ref, w_ref, out_ref, *, d):
    gate = cond_ref[:, 2 * d:3 * d]
    y = (o_ref[...].astype(F32) * _silu(g_ref[...].astype(F32))).astype(BF16)
    x2 = x_ref[...] + gate * _dot(y, w_ref[...])
    out_ref[...] = _normed(x2) * fgain_ref[...]


def _sb_out(o2, g, x1, cond, fgain, w_out, seq, tm=512):
    n, d = x1.shape
    tpb = seq // tm
    row = lambda i: (i, 0)
    fixed = lambda i: (0, 0)
    return pl.pallas_call(
        functools.partial(_sb_out_kernel, d=d),
        grid=(n // tm,),
        in_specs=[pl.BlockSpec((tm, d), row), pl.BlockSpec((tm, d), row), pl.BlockSpec((tm, d), row),
                  pl.BlockSpec((None, 1, 3 * d), lambda i: (i // tpb, 0, 0)),
                  pl.BlockSpec((1, d), fixed), pl.BlockSpec((d, d), fixed)],
        out_specs=pl.BlockSpec((tm, d), row),
        out_shape=jax.ShapeDtypeStruct((n, d), F32),
        compiler_params=_params(1),
        name="sb_out",
    )(o2, g, x1, cond, fgain, w_out)


def kernel(x, c, norm_gain, w_ada, b_ada, gla_w_in, gla_w_gk2, gla_b_gk, gla_o_gain, gla_w_out,
           kv_gain, kv_w_ada, kv_b_ada, w_kv, sb_w_in, sb_w_out, final_gain):
    bsz, seq, d = x.shape
    assert w_ada.shape[0] == 2 and gla_w_in.shape[0] == 1 and sb_w_in.shape[0] == 1
    dk = gla_w_gk2.shape[-1]
    n = bsz * seq
    x2 = x.reshape(n, d)

    cond = _ada(c, w_ada, b_ada)
    cond_kv = _ada(c, kv_w_ada[None], kv_b_ada[None])[0]
    cond1 = cond[0].reshape(bsz, 1, 3 * d)
    cond2 = cond[1].reshape(bsz, 1, 3 * d)
    cond_kv = cond_kv.reshape(bsz, 1, 2 * d)

    w_in = gla_w_in[0]
    n_main = 2 * dk + 2 * d
    w_main = w_in[:, :n_main].astype(BF16)
    w_lr = jnp.pad(w_in[:, n_main:], ((0, 0), (0, LANES - GLA_GATE_RANK))).astype(BF16)
    w_gk2 = jnp.pad(gla_w_gk2[0], ((0, LANES - GLA_GATE_RANK), (0, 0))).astype(BF16)
    qe, ke, v, g, dec = _gla_proj(x2, cond1, norm_gain[0:1], w_main, w_lr, w_gk2,
                                  gla_b_gk[0:1], seq)
    x1 = _gla_core(qe, ke, v, dec, g, x2, cond1, gla_o_gain[0:1], gla_w_out[0].astype(BF16), bsz, seq)

    dh = d // SB_HEADS
    qscale = (dh ** -0.5) * LOG2E
    ks, vs, qs, g2 = _kvq_proj(x1, cond_kv, cond2, kv_gain.reshape(1, d), norm_gain[1:2],
                               w_kv.astype(BF16), sb_w_in[0].astype(BF16), seq, qscale)
    o2 = _sb_attn(qs, ks, vs, bsz, seq)
    out = _sb_out(o2.reshape(n, d), g2, x1, cond2, final_gain.reshape(1, d),
                  sb_w_out[0].astype(BF16), seq)
    return out.reshape(bsz, seq, d)
```

```python
import functools

import jax
import jax.numpy as jnp
from jax import lax
from jax.experimental import pallas as pl
from jax.experimental.pallas import tpu as pltpu

F32 = jnp.float32
BF16 = jnp.bfloat16

RMS_EPS = 1e-6
GLA_HEADS = 4
GLA_CHUNK = 64
GLA_GATE_RANK = 16
GLA_GATE_NORMALIZER = 16.0
SB_HEADS = 16
LANES = 128
MXU_TILE = 256
LOG2E = 1.4426950408889634
VMEM_LIMIT = 48 * 1024 * 1024
SB_SKIP_LOG2 = -160.0
SB_MASKED_Z = -1e30


def _dot(a, b):
    return jnp.dot(a, b, preferred_element_type=F32)


def _dot_nt(a, b):
    return lax.dot_general(a, b, (((1,), (1,)), ((), ())), preferred_element_type=F32)


def _silu(x):
    return x * jax.nn.sigmoid(x)


def _normed(x):
    return x * lax.rsqrt(jnp.mean(x * x, axis=-1, keepdims=True) + RMS_EPS)


def _split_bf16(x):
    hi = x.astype(BF16)
    return hi, (x - hi.astype(F32)).astype(BF16)


def _params(n_axes):
    return pltpu.CompilerParams(dimension_semantics=("arbitrary",) * n_axes,
                                vmem_limit_bytes=VMEM_LIMIT)


def _ada_kernel(c_ref, w_ref, b_ref, o_ref):
    hi, lo = _split_bf16(_silu(c_ref[...]))
    w = w_ref[...].astype(BF16)
    o_ref[...] = _dot(hi, w) + _dot(lo, w) + b_ref[...]


def _ada(c, w, b, tn=512):
    n_l, d, n = w.shape
    bsz = c.shape[0]
    return pl.pallas_call(
        _ada_kernel,
        grid=(n_l, n // tn),
        in_specs=[pl.BlockSpec((bsz, d), lambda l, j: (0, 0)),
                  pl.BlockSpec((None, d, tn), lambda l, j: (l, 0, j)),
                  pl.BlockSpec((None, 1, tn), lambda l, j: (l, 0, j))],
        out_specs=pl.BlockSpec((None, bsz, tn), lambda l, j: (l, 0, j)),
        out_shape=jax.ShapeDtypeStruct((n_l, bsz, n), F32),
        compiler_params=_params(2),
        name="ada",
    )(c, w, b.reshape(n_l, 1, n))


def _gla_proj_kernel(x_ref, cond_ref, gain_ref, w_ref, wlr_ref, wgk_ref, bgk_ref,
                     qe_ref, ke_ref, v_ref, g_ref, dec_ref, w_s, *, d, dk, tm, qscale):
    n_main = 2 * dk + 2 * d

    @pl.when(pl.program_id(0) == 0)
    def _():
        w_s[...] = w_ref[:, 0:n_main].astype(BF16)

    c = GLA_CHUNK
    half = MXU_TILE
    shift = cond_ref[:, 0:d]
    scale = cond_ref[:, d:2 * d]
    h = (_normed(x_ref[...]) * gain_ref[...] * (1.0 + scale) + shift).astype(BF16)
    lr = _dot(h, wlr_ref[...]).astype(BF16)
    pre = _dot(lr, wgk_ref[...]) + bgk_ref[...]
    q = _dot(h, w_s[:, 0:dk])
    k = _dot(h, w_s[:, dk:2 * dk])
    gk = jax.nn.log_sigmoid(pre) * (1.0 / GLA_GATE_NORMALIZER)
    hi, lo = _split_bf16(gk)

    ri = lax.broadcasted_iota(jnp.int32, (half, half), 0)
    ci = lax.broadcasted_iota(jnp.int32, (half, half), 1)
    same_chunk = (ri // c) == (ci // c)
    tril = jnp.where(jnp.logical_and(same_chunk, ci <= ri), 1.0, 0.0).astype(BF16)
    tril2 = jnp.concatenate([tril, tril], axis=1)
    n_c = tm // c
    cj = lax.broadcasted_iota(jnp.int32, (n_c, tm), 1) // c
    cr = lax.broadcasted_iota(jnp.int32, (n_c, tm), 0)
    sel = jnp.where(cj == cr, 1.0, 0.0).astype(BF16)
    b_last = _dot(jnp.concatenate([sel, sel], axis=1), jnp.concatenate([hi, lo], axis=0))
    dec_ref[...] = jnp.exp(b_last)
    bs = [_dot(tril2, jnp.concatenate([hi[s * half:(s + 1) * half], lo[s * half:(s + 1) * half]], axis=0))
          for s in range(tm // half)]
    v_ref[...] = _dot(h, w_s[:, 2 * dk:2 * dk + d]).astype(BF16)
    g_ref[...] = _dot(h, w_s[:, 2 * dk + d:2 * dk + 2 * d]).astype(BF16)
    for s, b in enumerate(bs):
        rows = slice(s * half, (s + 1) * half)
        qe_ref[rows, :] = (q[rows] * qscale * jnp.exp(b)).astype(BF16)
        ke_ref[rows, :] = (k[rows] * jnp.exp(-b)).astype(BF16)


def _gla_proj(x2, cond, gain, w_in, w_lr, w_gk2, b_gk, seq, tm=512):
    n, d = x2.shape
    dk = w_gk2.shape[1]
    tpb = seq // tm
    n_c = tm // GLA_CHUNK
    row = lambda i: (i, 0)
    fixed = lambda i: (0, 0)
    qscale = (dk // GLA_HEADS) ** -0.5
    return pl.pallas_call(
        functools.partial(_gla_proj_kernel, d=d, dk=dk, tm=tm, qscale=qscale),
        grid=(n // tm,),
        in_specs=[pl.BlockSpec((tm, d), row),
                  pl.BlockSpec((None, 1, 3 * d), lambda i: (i // tpb, 0, 0)),
                  pl.BlockSpec((1, d), fixed),
                  pl.BlockSpec(w_in.shape, fixed, pipeline_mode=pl.Buffered(1)),
                  pl.BlockSpec(w_lr.shape, fixed),
                  pl.BlockSpec(w_gk2.shape, fixed),
                  pl.BlockSpec((1, dk), fixed)],
        out_specs=[pl.BlockSpec((tm, dk), row), pl.BlockSpec((tm, dk), row),
                   pl.BlockSpec((tm, d), row), pl.BlockSpec((tm, d), row),
                   pl.BlockSpec((n_c, dk), row)],
        out_shape=[jax.ShapeDtypeStruct((n, dk), BF16), jax.ShapeDtypeStruct((n, dk), BF16),
                   jax.ShapeDtypeStruct((n, d), BF16), jax.ShapeDtypeStruct((n, d), BF16),
                   jax.ShapeDtypeStruct((n // GLA_CHUNK, dk), F32)],
        scratch_shapes=[pltpu.VMEM((d, 2 * dk + 2 * d), BF16)],
        compiler_params=_params(1),
        name="gla_proj",
    )(x2, cond, gain, w_in, w_lr, w_gk2, b_gk)


def _gla_core_kernel(qe_ref, ke_ref, v_ref, dec_ref, g_ref, x_ref, cond_ref, ogain_ref, w_ref,
                     x1_ref, s_ref, w_s, *, tile, tiles_per_seq, d, hk, hv):
    c = GLA_CHUNK
    n_c = tile // c
    heads = range(GLA_HEADS)
    ri = lax.broadcasted_iota(jnp.int32, (c, c), 0)
    ci = lax.broadcasted_iota(jnp.int32, (c, c), 1)
    causal = ci <= ri

    @pl.when(pl.program_id(0) == 0)
    def _():
        w_s[...] = w_ref[...].astype(BF16)

    @pl.when(pl.program_id(0) % tiles_per_seq == 0)
    def _():
        s_ref[...] = jnp.zeros(s_ref.shape, F32)

    def chunk(ref, h, u, width):
        return ref[u * c:(u + 1) * c, h * width:(h + 1) * width]

    work = [(h, u) for u in range(n_c) for h in heads]
    qs = {hu: chunk(qe_ref, *hu, hk) for hu in work}
    ks = {hu: chunk(ke_ref, *hu, hk) for hu in work}
    vs = {hu: chunk(v_ref, *hu, hv) for hu in work}
    scores = {hu: _dot_nt(qs[hu], ks[hu]) for hu in work}
    incs = {hu: _dot(ks[hu].T, vs[hu]) for hu in work}
    masked = {hu: jnp.where(causal, scores[hu], 0.0).astype(BF16) for hu in work}
    dec_t = [dec_ref[:, h * hk:(h + 1) * hk].T for h in heads]
    s = [s_ref[h] for h in heads]
    o = {}
    for h, u in work:
        lhs = jnp.concatenate([qs[h, u], masked[h, u]], axis=1)
        rhs = jnp.concatenate([s[h].astype(BF16), vs[h, u]], axis=0)
        o[h, u] = _dot(lhs, rhs)
        s[h] = (s[h] + incs[h, u]) * dec_t[h][:, u:u + 1]
    for h in heads:
        s_ref[h] = s[h]

    gate = cond_ref[:, 2 * d:3 * d]
    for u in range(n_c):
        rows = slice(u * c, (u + 1) * c)
        y = jnp.concatenate([_normed(o[h, u]) * ogain_ref[...] for h in heads], axis=1)
        y = (y * _silu(g_ref[rows, :].astype(F32))).astype(BF16)
        x1_ref[rows, :] = x_ref[rows, :] + gate * _dot(y, w_s[...])


def _gla_core(qe, ke, v, dec, g, x2, cond, ogain, w_out, bsz, seq, tile=512):
    d = x2.shape[-1]
    dk = qe.shape[-1]
    n_c = tile // GLA_CHUNK
    tpb = seq // tile
    row = lambda i: (i, 0)
    fixed = lambda i: (0, 0)
    return pl.pallas_call(
        functools.partial(_gla_core_kernel, tile=tile, tiles_per_seq=tpb, d=d, hk=dk // GLA_HEADS,
                          hv=d // GLA_HEADS),
        grid=(bsz * tpb,),
        in_specs=[pl.BlockSpec((tile, dk), row), pl.BlockSpec((tile, dk), row),
                  pl.BlockSpec((tile, d), row), pl.BlockSpec((n_c, dk), row),
                  pl.BlockSpec((tile, d), row), pl.BlockSpec((tile, d), row),
                  pl.BlockSpec((None, 1, 3 * d), lambda i: (i // tpb, 0, 0)),
                  pl.BlockSpec((1, d // GLA_HEADS), fixed),
                  pl.BlockSpec((d, d), fixed, pipeline_mode=pl.Buffered(1))],
        out_specs=pl.BlockSpec((tile, d), row),
        out_shape=jax.ShapeDtypeStruct((bsz * seq, d), F32),
        scratch_shapes=[pltpu.VMEM((GLA_HEADS, dk // GLA_HEADS, d // GLA_HEADS), F32),
                        pltpu.VMEM((d, d), BF16)],
        compiler_params=_params(1),
        name="gla_core",
    )(qe, ke, v, dec, g, x2, cond, ogain, w_out)


def _kvq_proj_kernel(x_ref, ckv_ref, c2_ref, kvgain_ref, gain_ref, wkv_ref, win_ref,
                     k_ref, v_ref, q_ref, g_ref, wkv_s, win_s, *, d, qscale):
    @pl.when(pl.program_id(0) == 0)
    def _():
        wkv_s[...] = wkv_ref[...].astype(BF16)
        win_s[...] = win_ref[...].astype(BF16)

    y = _normed(x_ref[...])
    hkv = (y * kvgain_ref[...] * (1.0 + ckv_ref[:, d:2 * d]) + ckv_ref[:, 0:d]).astype(BF16)
    h2 = (y * gain_ref[...] * (1.0 + c2_ref[:, d:2 * d]) + c2_ref[:, 0:d]).astype(BF16)
    k_ref[...] = _dot(hkv, wkv_s[:, 0:d]).astype(BF16)
    v_ref[...] = _dot(hkv, wkv_s[:, d:2 * d]).astype(BF16)
    q_ref[...] = (_dot(h2, win_s[:, 0:d]) * qscale).astype(BF16)
    g_ref[...] = _dot(h2, win_s[:, d:2 * d]).astype(BF16)


def _kvq_proj(x1, ckv, c2, kvgain, gain, w_kv, w_in, seq, qscale, tm=512):
    n, d = x1.shape
    tpb = seq // tm
    row = lambda i: (i, 0)
    fixed = lambda i: (0, 0)
    once = dict(pipeline_mode=pl.Buffered(1))
    return pl.pallas_call(
        functools.partial(_kvq_proj_kernel, d=d, qscale=qscale),
        grid=(n // tm,),
        in_specs=[pl.BlockSpec((tm, d), row),
                  pl.BlockSpec((None, 1, 2 * d), lambda i: (i // tpb, 0, 0)),
                  pl.BlockSpec((None, 1, 3 * d), lambda i: (i // tpb, 0, 0)),
                  pl.BlockSpec((1, d), fixed), pl.BlockSpec((1, d), fixed),
                  pl.BlockSpec((d, 2 * d), fixed, **once), pl.BlockSpec((d, 2 * d), fixed, **once)],
        out_specs=[pl.BlockSpec((tm, d), row)] * 4,
        out_shape=[jax.ShapeDtypeStruct((n, d), BF16)] * 4,
        scratch_shapes=[pltpu.VMEM((d, 2 * d), BF16), pltpu.VMEM((d, 2 * d), BF16)],
        compiler_params=_params(1),
        name="kvq_proj",
    )(x1, ckv, c2, kvgain, gain, w_kv, w_in)


def _sb_attn_kernel(q_ref, k_ref, v_ref, o_ref, *, seq, blk, dh, pairs):
    n_blk = seq // blk
    pw = 2 * dh
    lane = lax.broadcasted_iota(jnp.int32, (blk, pw), 1)
    ri = lax.broadcasted_iota(jnp.int32, (blk, blk), 0)
    ci = lax.broadcasted_iota(jnp.int32, (blk, blk), 1)
    strict = jnp.where(ci < ri, 1.0, 0.0).astype(BF16)
    ri2 = lax.broadcasted_iota(jnp.int32, (2 * blk, blk), 0)
    ci2 = lax.broadcasted_iota(jnp.int32, (2 * blk, blk), 1)
    below = ci2 < (ri2 & (blk - 1))

    def rows(j):
        return pl.ds(pl.multiple_of(j * blk, blk), blk)

    def stacked_q(i, p):
        q2 = q_ref[rows(i), p * pw:(p + 1) * pw]
        zero = jnp.zeros_like(q2)
        return jnp.concatenate([jnp.where(lane < dh, q2, zero), jnp.where(lane >= dh, q2, zero)], axis=0)

    def run_blocks(probs, js, diag_first, carries):
        work = [(p, n) for n in range(len(js)) for p in range(len(probs))]
        zs = {(p, n): _dot_nt(probs[p][1], k_ref[rows(js[n]), probs[p][0] * pw:(probs[p][0] + 1) * pw])
              for p, n in work}
        lbs, lms = {}, {}
        for key in work:
            z = zs[key]
            if diag_first and key[1] == 0:
                z = jnp.where(below, z, SB_MASKED_Z)
            lg = jnp.log2(1.0 + jnp.exp2(-jnp.abs(z)))
            lb = jnp.minimum(z, 0.0) - lg
            lbs[key], lms[key] = lb, lb - z
        sums = {key: _dot(lms[key].astype(BF16), strict) for key in work}
        carries = list(carries)
        ws = {}
        for p, n in work:
            cexc = sums[p, n]
            if carries[p] is not None:
                cexc = cexc + jnp.concatenate([carries[p]] * (blk // LANES), axis=1)
            ws[p, n] = jnp.exp2(lbs[p, n] + cexc).astype(BF16)
            carries[p] = jnp.broadcast_to(cexc[:, 0:1] + lms[p, n][:, 0:1], (2 * blk, LANES))
        accs = []
        for p in range(len(probs)):
            lanes = slice(probs[p][0] * pw, (probs[p][0] + 1) * pw)
            acc = _dot(ws[p, 0], v_ref[rows(js[0]), lanes])
            for n in range(1, len(js)):
                acc = acc + _dot(ws[p, n], v_ref[rows(js[n]), lanes])
            accs.append(acc)
        return accs, carries

    def finish(i, p, acc):
        o_ref[rows(i), p * pw:(p + 1) * pw] = jnp.where(lane < dh, acc[:blk], acc[blk:]).astype(o_ref.dtype)

    probs0 = [(p, stacked_q(0, p)) for p in range(pairs)]
    accs0, _ = run_blocks(probs0, [0], True, [None] * pairs)
    for p in range(pairs):
        finish(0, p, accs0[p])

    def query_block(i, _):
        probs = [(p, stacked_q(i, p)) for p in range(pairs)]
        accs, carries = run_blocks(probs, [i, i - 1], True, [None] * pairs)
        for p in range(pairs):
            def cond(st):
                j, cmax, _, _ = st
                return jnp.logical_and(j >= 0, cmax >= SB_SKIP_LOG2)

            def body(st, p=p):
                j, _, carry, acc = st
                pv, cs = run_blocks([probs[p]], [j], False, [carry])
                return j - 1, jnp.max(cs[0]), cs[0], acc + pv[0]

            st = lax.while_loop(cond, body, (i - 2, jnp.max(carries[p]), carries[p], accs[p]))
            finish(i, p, st[3])
        return 0

    lax.fori_loop(1, n_blk, query_block, 0)


def _sb_attn(q, k, v, bsz, seq, blk=256, pairs=2):
    d = q.shape[-1]
    dh = d // SB_HEADS
    gw = pairs * 2 * dh
    q3, k3, v3 = (t.reshape(bsz, seq, d) for t in (q, k, v))
    spec = pl.BlockSpec((None, seq, gw), lambda b, p: (b, 0, p))
    return pl.pallas_call(
        functools.partial(_sb_attn_kernel, seq=seq, blk=blk, dh=dh, pairs=pairs),
        grid=(bsz, d // gw),
        in_specs=[spec, spec, spec],
        out_specs=spec,
        out_shape=jax.ShapeDtypeStruct((bsz, seq, d), BF16),
        compiler_params=_params(2),
        name="sb_attn",
    )(q3, k3, v3)


def _sb_out_kernel(o_ref, g_ref, x_ref, cond_ref, fgain_ref, w_ref, out_ref, w_s, *, d):
    @pl.when(pl.program_id(0) == 0)
    def _():
        w_s[...] = w_ref[...].astype(BF16)

    gate = cond_ref[:, 2 * d:3 * d]
    y = (o_ref[...].astype(F32) * _silu(g_ref[...].astype(F32))).astype(BF16)
    x2 = x_ref[...] + gate * _dot(y, w_s[...])
    out_ref[...] = _normed(x2) * fgain_ref[...]


def _sb_out(o2, g, x1, cond, fgain, w_out, seq, tm=1024):
    n, d = x1.shape
    tpb = seq // tm
    row = lambda i: (i, 0)
    fixed = lambda i: (0, 0)
    return pl.pallas_call(
        functools.partial(_sb_out_kernel, d=d),
        grid=(n // tm,),
        in_specs=[pl.BlockSpec((tm, d), row), pl.BlockSpec((tm, d), row), pl.BlockSpec((tm, d), row),
                  pl.BlockSpec((None, 1, 3 * d), lambda i: (i // tpb, 0, 0)),
                  pl.BlockSpec((1, d), fixed),
                  pl.BlockSpec((d, d), fixed, pipeline_mode=pl.Buffered(1))],
        out_specs=pl.BlockSpec((tm, d), row),
        out_shape=jax.ShapeDtypeStruct((n, d), F32),
        scratch_shapes=[pltpu.VMEM((d, d), BF16)],
        compiler_params=_params(1),
        name="sb_out",
    )(o2, g, x1, cond, fgain, w_out)


def kernel(x, c, norm_gain, w_ada, b_ada, gla_w_in, gla_w_gk2, gla_b_gk, gla_o_gain, gla_w_out,
           kv_gain, kv_w_ada, kv_b_ada, w_kv, sb_w_in, sb_w_out, final_gain):
    bsz, seq, d = x.shape
    assert w_ada.shape[0] == 2 and gla_w_in.shape[0] == 1 and sb_w_in.shape[0] == 1
    dk = gla_w_gk2.shape[-1]
    n = bsz * seq
    x2 = x.reshape(n, d)

    cond = _ada(c, w_ada, b_ada)
    cond_kv = _ada(c, kv_w_ada[None], kv_b_ada[None])[0]
    cond1 = cond[0].reshape(bsz, 1, 3 * d)
    cond2 = cond[1].reshape(bsz, 1, 3 * d)
    cond_kv = cond_kv.reshape(bsz, 1, 2 * d)

    w_in = gla_w_in[0]
    n_main = 2 * dk + 2 * d
    w_lr = jnp.pad(w_in[:, n_main:], ((0, 0), (0, LANES - GLA_GATE_RANK))).astype(BF16)
    w_gk2 = jnp.pad(gla_w_gk2[0], ((0, LANES - GLA_GATE_RANK), (0, 0))).astype(BF16)
    qe, ke, v, g, dec = _gla_proj(x2, cond1, norm_gain[0:1], w_in, w_lr, w_gk2,
                                  gla_b_gk[0:1], seq)
    x1 = _gla_core(qe, ke, v, dec, g, x2, cond1, gla_o_gain[0:1], gla_w_out[0], bsz, seq)

    dh = d // SB_HEADS
    qscale = (dh ** -0.5) * LOG2E
    ks, vs, qs, g2 = _kvq_proj(x1, cond_kv, cond2, kv_gain.reshape(1, d), norm_gain[1:2],
                               w_kv, sb_w_in[0], seq, qscale)
    o2 = _sb_attn(qs, ks, vs, bsz, seq)
    out = _sb_out(o2.reshape(n, d), g2, x1, cond2, final_gain.reshape(1, d),
                  sb_w_out[0], seq)
    return out.reshape(bsz, seq, d)
```

```python
import functools

import jax
import jax.numpy as jnp
from jax import lax
from jax.experimental import pallas as pl
from jax.experimental.pallas import tpu as pltpu

F32 = jnp.float32
BF16 = jnp.bfloat16

RMS_EPS = 1e-6
GLA_HEADS = 4
GLA_CHUNK = 64
GLA_GATE_RANK = 16
GLA_GATE_NORMALIZER = 16.0
SB_HEADS = 16
LANES = 128
MXU_TILE = 256
LOG2E = 1.4426950408889634
VMEM_LIMIT = 48 * 1024 * 1024
SB_SKIP_LOG2 = 160.0
SB_MASKED_Z = -1e30
SB_EXP2_CLAMP = 126.0


def _dot(a, b):
    return jnp.dot(a, b, preferred_element_type=F32)


def _dot_nt(a, b):
    return lax.dot_general(a, b, (((1,), (1,)), ((), ())), preferred_element_type=F32)


def _silu(x):
    return x * jax.nn.sigmoid(x)


def _normed(x):
    return x * lax.rsqrt(jnp.mean(x * x, axis=-1, keepdims=True) + RMS_EPS)


def _split_bf16(x):
    hi = x.astype(BF16)
    return hi, (x - hi.astype(F32)).astype(BF16)


def _params(n_axes):
    return pltpu.CompilerParams(dimension_semantics=("arbitrary",) * n_axes,
                                vmem_limit_bytes=VMEM_LIMIT)


def _ada_kernel(c_ref, w_ref, b_ref, o_ref):
    hi, lo = _split_bf16(_silu(c_ref[...]))
    w = w_ref[...].astype(BF16)
    o_ref[...] = _dot(hi, w) + _dot(lo, w) + b_ref[...]


def _ada(c, w, b, tn=512):
    n_l, d, n = w.shape
    bsz = c.shape[0]
    return pl.pallas_call(
        _ada_kernel,
        grid=(n_l, n // tn),
        in_specs=[pl.BlockSpec((bsz, d), lambda l, j: (0, 0)),
                  pl.BlockSpec((None, d, tn), lambda l, j: (l, 0, j)),
                  pl.BlockSpec((None, 1, tn), lambda l, j: (l, 0, j))],
        out_specs=pl.BlockSpec((None, bsz, tn), lambda l, j: (l, 0, j)),
        out_shape=jax.ShapeDtypeStruct((n_l, bsz, n), F32),
        compiler_params=_params(2),
        name="ada",
    )(c, w, b.reshape(n_l, 1, n))


def _gla_proj_kernel(x_ref, cond_ref, gain_ref, w_ref, wlr_ref, wgk_ref, bgk_ref,
                     qe_ref, ke_ref, v_ref, g_ref, dec_ref, w_s, *, d, dk, tm, qscale):
    n_main = 2 * dk + 2 * d

    @pl.when(pl.program_id(0) == 0)
    def _():
        w_s[...] = w_ref[0:n_main, :].astype(BF16)

    c = GLA_CHUNK
    half = MXU_TILE
    shift = cond_ref[:, 0:d]
    scale = cond_ref[:, d:2 * d]
    h = (_normed(x_ref[...]) * gain_ref[...] * (1.0 + scale) + shift).astype(BF16)
    lr = _dot_nt(h, wlr_ref[...]).astype(BF16)
    pre = _dot(lr, wgk_ref[...]) + bgk_ref[...]
    q = _dot_nt(h, w_s[0:dk, :])
    k = _dot_nt(h, w_s[dk:2 * dk, :])
    gk = jax.nn.log_sigmoid(pre) * (1.0 / GLA_GATE_NORMALIZER)
    hi, lo = _split_bf16(gk)

    ri = lax.broadcasted_iota(jnp.int32, (half, half), 0)
    ci = lax.broadcasted_iota(jnp.int32, (half, half), 1)
    same_chunk = (ri // c) == (ci // c)
    tril = jnp.where(jnp.logical_and(same_chunk, ci <= ri), 1.0, 0.0).astype(BF16)
    tril2 = jnp.concatenate([tril, tril], axis=1)
    n_c = tm // c
    cj = lax.broadcasted_iota(jnp.int32, (n_c, tm), 1) // c
    cr = lax.broadcasted_iota(jnp.int32, (n_c, tm), 0)
    sel = jnp.where(cj == cr, 1.0, 0.0).astype(BF16)
    b_last = _dot(jnp.concatenate([sel, sel], axis=1), jnp.concatenate([hi, lo], axis=0))
    dec_ref[...] = jnp.exp(b_last)
    bs = [_dot(tril2, jnp.concatenate([hi[s * half:(s + 1) * half], lo[s * half:(s + 1) * half]], axis=0))
          for s in range(tm // half)]
    v_ref[...] = _dot_nt(h, w_s[2 * dk:2 * dk + d, :]).astype(BF16)
    g_ref[...] = _dot_nt(h, w_s[2 * dk + d:2 * dk + 2 * d, :]).astype(BF16)
    for s, b in enumerate(bs):
        rows = slice(s * half, (s + 1) * half)
        qe_ref[rows, :] = (q[rows] * qscale * jnp.exp(b)).astype(BF16)
        ke_ref[rows, :] = (k[rows] * jnp.exp(-b)).astype(BF16)


def _gla_proj(x2, cond, gain, w_in, w_lr, w_gk2, b_gk, seq, tm=512):
    n, d = x2.shape
    dk = w_gk2.shape[1]
    tpb = seq // tm
    n_c = tm // GLA_CHUNK
    row = lambda i: (i, 0)
    fixed = lambda i: (0, 0)
    qscale = (dk // GLA_HEADS) ** -0.5
    return pl.pallas_call(
        functools.partial(_gla_proj_kernel, d=d, dk=dk, tm=tm, qscale=qscale),
        grid=(n // tm,),
        in_specs=[pl.BlockSpec((tm, d), row),
                  pl.BlockSpec((None, 1, 3 * d), lambda i: (i // tpb, 0, 0)),
                  pl.BlockSpec((1, d), fixed),
                  pl.BlockSpec(w_in.shape, fixed, pipeline_mode=pl.Buffered(1)),
                  pl.BlockSpec(w_lr.shape, fixed),
                  pl.BlockSpec(w_gk2.shape, fixed),
                  pl.BlockSpec((1, dk), fixed)],
        out_specs=[pl.BlockSpec((tm, dk), row), pl.BlockSpec((tm, dk), row),
                   pl.BlockSpec((tm, d), row), pl.BlockSpec((tm, d), row),
                   pl.BlockSpec((n_c, dk), row)],
        out_shape=[jax.ShapeDtypeStruct((n, dk), BF16), jax.ShapeDtypeStruct((n, dk), BF16),
                   jax.ShapeDtypeStruct((n, d), BF16), jax.ShapeDtypeStruct((n, d), BF16),
                   jax.ShapeDtypeStruct((n // GLA_CHUNK, dk), F32)],
        scratch_shapes=[pltpu.VMEM((2 * dk + 2 * d, d), BF16)],
        compiler_params=_params(1),
        name="gla_proj",
    )(x2, cond, gain, w_in, w_lr, w_gk2, b_gk)


def _gla_core_kernel(qe_ref, ke_ref, v_ref, dec_ref, g_ref, x_ref, cond_ref, ogain_ref, w_ref,
                     x1_ref, s_ref, w_s, y_s, *, tile, tiles_per_seq, d, hk, hv):
    c = GLA_CHUNK
    n_c = tile // c
    heads = range(GLA_HEADS)
    ri = lax.broadcasted_iota(jnp.int32, (c, c), 0)
    ci = lax.broadcasted_iota(jnp.int32, (c, c), 1)
    causal = ci <= ri

    @pl.when(pl.program_id(0) == 0)
    def _():
        w_s[...] = w_ref[...].astype(BF16)

    @pl.when(pl.program_id(0) % tiles_per_seq == 0)
    def _():
        s_ref[...] = jnp.zeros(s_ref.shape, F32)

    def chunk(ref, h, u, width):
        return ref[u * c:(u + 1) * c, h * width:(h + 1) * width]

    work = [(h, u) for u in range(n_c) for h in heads]
    qs = {hu: chunk(qe_ref, *hu, hk) for hu in work}
    ks = {hu: chunk(ke_ref, *hu, hk) for hu in work}
    vs = {hu: chunk(v_ref, *hu, hv) for hu in work}
    scores = {hu: _dot_nt(qs[hu], ks[hu]) for hu in work}
    incs = {hu: _dot(ks[hu].T, vs[hu]) for hu in work}
    masked = {hu: jnp.where(causal, scores[hu], 0.0).astype(BF16) for hu in work}
    dec_t = [dec_ref[:, h * hk:(h + 1) * hk].T for h in heads]
    s = [s_ref[h] for h in heads]
    o = {}
    for h, u in work:
        lhs = jnp.concatenate([qs[h, u], masked[h, u]], axis=1)
        rhs = jnp.concatenate([s[h].astype(BF16), vs[h, u]], axis=0)
        o[h, u] = _dot(lhs, rhs)
        s[h] = (s[h] + incs[h, u]) * dec_t[h][:, u:u + 1]
    for h in heads:
        s_ref[h] = s[h]

    for u in range(n_c):
        rows = slice(u * c, (u + 1) * c)
        y = jnp.concatenate([_normed(o[h, u]) * ogain_ref[...] for h in heads], axis=1)
        y_s[rows, :] = (y * _silu(g_ref[rows, :].astype(F32))).astype(BF16)
    x1_ref[...] = x_ref[...] + cond_ref[:, 2 * d:3 * d] * _dot(y_s[...], w_s[...])


def _gla_core(qe, ke, v, dec, g, x2, cond, ogain, w_out, bsz, seq, tile=512):
    d = x2.shape[-1]
    dk = qe.shape[-1]
    n_c = tile // GLA_CHUNK
    tpb = seq // tile
    row = lambda i: (i, 0)
    fixed = lambda i: (0, 0)
    return pl.pallas_call(
        functools.partial(_gla_core_kernel, tile=tile, tiles_per_seq=tpb, d=d, hk=dk // GLA_HEADS,
                          hv=d // GLA_HEADS),
        grid=(bsz * tpb,),
        in_specs=[pl.BlockSpec((tile, dk), row), pl.BlockSpec((tile, dk), row),
                  pl.BlockSpec((tile, d), row), pl.BlockSpec((n_c, dk), row),
                  pl.BlockSpec((tile, d), row), pl.BlockSpec((tile, d), row),
                  pl.BlockSpec((None, 1, 3 * d), lambda i: (i // tpb, 0, 0)),
                  pl.BlockSpec((1, d // GLA_HEADS), fixed),
                  pl.BlockSpec((d, d), fixed, pipeline_mode=pl.Buffered(1))],
        out_specs=pl.BlockSpec((tile, d), row),
        out_shape=jax.ShapeDtypeStruct((bsz * seq, d), F32),
        scratch_shapes=[pltpu.VMEM((GLA_HEADS, dk // GLA_HEADS, d // GLA_HEADS), F32),
                        pltpu.VMEM((d, d), BF16), pltpu.VMEM((tile, d), BF16)],
        compiler_params=_params(1),
        name="gla_core",
    )(qe, ke, v, dec, g, x2, cond, ogain, w_out)


def _kvq_proj_kernel(x_ref, ckv_ref, c2_ref, kvgain_ref, gain_ref, wkv_ref, win_ref,
                     k_ref, v_ref, q_ref, g_ref, wkv_s, win_s, *, d, qscale):
    @pl.when(pl.program_id(0) == 0)
    def _():
        wkv_s[...] = wkv_ref[...].astype(BF16)
        win_s[...] = win_ref[...].astype(BF16)

    y = _normed(x_ref[...])
    hkv = (y * kvgain_ref[...] * (1.0 + ckv_ref[:, d:2 * d]) + ckv_ref[:, 0:d]).astype(BF16)
    h2 = (y * gain_ref[...] * (1.0 + c2_ref[:, d:2 * d]) + c2_ref[:, 0:d]).astype(BF16)
    k_ref[...] = _dot(hkv, wkv_s[:, 0:d]).astype(BF16)
    v_ref[...] = _dot(hkv, wkv_s[:, d:2 * d]).astype(BF16)
    q_ref[...] = (_dot(h2, win_s[:, 0:d]) * qscale).astype(BF16)
    g_ref[...] = _dot(h2, win_s[:, d:2 * d]).astype(BF16)


def _kvq_proj(x1, ckv, c2, kvgain, gain, w_kv, w_in, seq, qscale, tm=512):
    n, d = x1.shape
    tpb = seq // tm
    row = lambda i: (i, 0)
    fixed = lambda i: (0, 0)
    once = dict(pipeline_mode=pl.Buffered(1))
    return pl.pallas_call(
        functools.partial(_kvq_proj_kernel, d=d, qscale=qscale),
        grid=(n // tm,),
        in_specs=[pl.BlockSpec((tm, d), row),
                  pl.BlockSpec((None, 1, 2 * d), lambda i: (i // tpb, 0, 0)),
                  pl.BlockSpec((None, 1, 3 * d), lambda i: (i // tpb, 0, 0)),
                  pl.BlockSpec((1, d), fixed), pl.BlockSpec((1, d), fixed),
                  pl.BlockSpec((d, 2 * d), fixed, **once), pl.BlockSpec((d, 2 * d), fixed, **once)],
        out_specs=[pl.BlockSpec((tm, d), row)] * 4,
        out_shape=[jax.ShapeDtypeStruct((n, d), BF16)] * 4,
        scratch_shapes=[pltpu.VMEM((d, 2 * d), BF16), pltpu.VMEM((d, 2 * d), BF16)],
        compiler_params=_params(1),
        name="kvq_proj",
    )(x1, ckv, c2, kvgain, gain, w_kv, w_in)


def _sb_attn_kernel(q_ref, k_ref, v_ref, o_ref, *, seq, blk, dh, pairs):
    n_blk = seq // blk
    pw = 2 * dh
    lane = lax.broadcasted_iota(jnp.int32, (blk, pw), 1)
    ri = lax.broadcasted_iota(jnp.int32, (blk, blk), 0)
    ci = lax.broadcasted_iota(jnp.int32, (blk, blk), 1)
    strict = jnp.where(ci < ri, 1.0, 0.0).astype(BF16)
    ri2 = lax.broadcasted_iota(jnp.int32, (2 * blk, blk), 0)
    ci2 = lax.broadcasted_iota(jnp.int32, (2 * blk, blk), 1)
    below = ci2 < (ri2 & (blk - 1))

    def rows(j):
        return pl.ds(pl.multiple_of(j * blk, blk), blk)

    def stacked_q(i, p):
        q2 = q_ref[rows(i), p * pw:(p + 1) * pw]
        zero = jnp.zeros_like(q2)
        return jnp.concatenate([jnp.where(lane < dh, q2, zero), jnp.where(lane >= dh, q2, zero)], axis=0)

    def run_blocks(probs, js, diag_first, carries):
        work = [(p, n) for n in range(len(js)) for p in range(len(probs))]
        zs = {(p, n): _dot_nt(probs[p][1], k_ref[rows(js[n]), probs[p][0] * pw:(probs[p][0] + 1) * pw])
              for p, n in work}
        lbs, sps = {}, {}
        for key in work:
            z = zs[key]
            if diag_first and key[1] == 0:
                z = jnp.where(below, z, SB_MASKED_Z)
            sp = jnp.maximum(jnp.log2(1.0 + jnp.exp2(jnp.minimum(z, SB_EXP2_CLAMP))), z)
            lbs[key], sps[key] = z - sp, sp
        sums = {key: _dot(sps[key].astype(BF16), strict) for key in work}
        carries = list(carries)
        ws = {}
        for p, n in work:
            tot = sums[p, n]
            if carries[p] is not None:
                tot = tot + jnp.concatenate([carries[p]] * (blk // LANES), axis=1)
            ws[p, n] = jnp.exp2(lbs[p, n] - tot).astype(BF16)
            carries[p] = jnp.broadcast_to(tot[:, 0:1] + sps[p, n][:, 0:1], (2 * blk, LANES))
        accs = []
        for p in range(len(probs)):
            lanes = slice(probs[p][0] * pw, (probs[p][0] + 1) * pw)
            acc = _dot(ws[p, 0], v_ref[rows(js[0]), lanes])
            for n in range(1, len(js)):
                acc = acc + _dot(ws[p, n], v_ref[rows(js[n]), lanes])
            accs.append(acc)
        return accs, carries

    def finish(i, p, acc):
        o_ref[rows(i), p * pw:(p + 1) * pw] = jnp.where(lane < dh, acc[:blk], acc[blk:]).astype(o_ref.dtype)

    probs0 = [(p, stacked_q(0, p)) for p in range(pairs)]
    accs0, _ = run_blocks(probs0, [0], True, [None] * pairs)
    for p in range(pairs):
        finish(0, p, accs0[p])

    def query_block(i, _):
        probs = [(p, stacked_q(i, p)) for p in range(pairs)]
        accs, carries = run_blocks(probs, [i, i - 1], True, [None] * pairs)
        for p in range(pairs):
            def cond(st):
                j, cmin, _, _ = st
                return jnp.logical_and(j >= 0, cmin <= SB_SKIP_LOG2)

            def body(st, p=p):
                j, _, carry, acc = st
                pv, cs = run_blocks([probs[p]], [j], False, [carry])
                return j - 1, jnp.min(cs[0]), cs[0], acc + pv[0]

            st = lax.while_loop(cond, body, (i - 2, jnp.min(carries[p]), carries[p], accs[p]))
            finish(i, p, st[3])
        return 0

    lax.fori_loop(1, n_blk, query_block, 0)


def _sb_attn(q, k, v, bsz, seq, blk=256, pairs=2):
    d = q.shape[-1]
    dh = d // SB_HEADS
    gw = pairs * 2 * dh
    q3, k3, v3 = (t.reshape(bsz, seq, d) for t in (q, k, v))
    spec = pl.BlockSpec((None, seq, gw), lambda b, p: (b, 0, p))
    return pl.pallas_call(
        functools.partial(_sb_attn_kernel, seq=seq, blk=blk, dh=dh, pairs=pairs),
        grid=(bsz, d // gw),
        in_specs=[spec, spec, spec],
        out_specs=spec,
        out_shape=jax.ShapeDtypeStruct((bsz, seq, d), BF16),
        compiler_params=_params(2),
        name="sb_attn",
    )(q3, k3, v3)


def _sb_out_kernel(o_ref, g_ref, x_ref, cond_ref, fgain_ref, w_ref, out_ref, w_s, *, d):
    @pl.when(pl.program_id(0) == 0)
    def _():
        w_s[...] = w_ref[...].astype(BF16)

    gate = cond_ref[:, 2 * d:3 * d]
    y = (o_ref[...].astype(F32) * _silu(g_ref[...].astype(F32))).astype(BF16)
    x2 = x_ref[...] + gate * _dot(y, w_s[...])
    out_ref[...] = _normed(x2) * fgain_ref[...]


def _sb_out(o2, g, x1, cond, fgain, w_out, seq, tm=1024):
    n, d = x1.shape
    tpb = seq // tm
    row = lambda i: (i, 0)
    fixed = lambda i: (0, 0)
    return pl.pallas_call(
        functools.partial(_sb_out_kernel, d=d),
        grid=(n // tm,),
        in_specs=[pl.BlockSpec((tm, d), row), pl.BlockSpec((tm, d), row), pl.BlockSpec((tm, d), row),
                  pl.BlockSpec((None, 1, 3 * d), lambda i: (i // tpb, 0, 0)),
                  pl.BlockSpec((1, d), fixed),
                  pl.BlockSpec((d, d), fixed, pipeline_mode=pl.Buffered(1))],
        out_specs=pl.BlockSpec((tm, d), row),
        out_shape=jax.ShapeDtypeStruct((n, d), F32),
        scratch_shapes=[pltpu.VMEM((d, d), BF16)],
        compiler_params=_params(1),
        name="sb_out",
    )(o2, g, x1, cond, fgain, w_out)


def kernel(x, c, norm_gain, w_ada, b_ada, gla_w_in, gla_w_gk2, gla_b_gk, gla_o_gain, gla_w_out,
           kv_gain, kv_w_ada, kv_b_ada, w_kv, sb_w_in, sb_w_out, final_gain):
    bsz, seq, d = x.shape
    assert w_ada.shape[0] == 2 and gla_w_in.shape[0] == 1 and sb_w_in.shape[0] == 1
    dk = gla_w_gk2.shape[-1]
    n = bsz * seq
    x2 = x.reshape(n, d)

    cond = _ada(c, w_ada, b_ada)
    cond_kv = _ada(c, kv_w_ada[None], kv_b_ada[None])[0]
    cond1 = cond[0].reshape(bsz, 1, 3 * d)
    cond2 = cond[1].reshape(bsz, 1, 3 * d)
    cond_kv = cond_kv.reshape(bsz, 1, 2 * d)

    w_in = jnp.swapaxes(gla_w_in[0], 0, 1)
    n_main = 2 * dk + 2 * d
    w_lr = jnp.pad(w_in[n_main:], ((0, LANES - GLA_GATE_RANK), (0, 0))).astype(BF16)
    w_gk2 = jnp.pad(gla_w_gk2[0], ((0, LANES - GLA_GATE_RANK), (0, 0))).astype(BF16)
    qe, ke, v, g, dec = _gla_proj(x2, cond1, norm_gain[0:1], w_in, w_lr, w_gk2,
                                  gla_b_gk[0:1], seq)
    x1 = _gla_core(qe, ke, v, dec, g, x2, cond1, gla_o_gain[0:1], gla_w_out[0], bsz, seq)

    dh = d // SB_HEADS
    qscale = (dh ** -0.5) * LOG2E
    ks, vs, qs, g2 = _kvq_proj(x1, cond_kv, cond2, kv_gain.reshape(1, d), norm_gain[1:2],
                               w_kv, sb_w_in[0], seq, qscale)
    o2 = _sb_attn(qs, ks, vs, bsz, seq)
    out = _sb_out(o2.reshape(n, d), g2, x1, cond2, final_gain.reshape(1, d),
                  sb_w_out[0], seq)
    return out.reshape(bsz, seq, d)
```

```python
import functools

import jax
import jax.numpy as jnp
from jax import lax
from jax.experimental import pallas as pl
from jax.experimental.pallas import tpu as pltpu

F32 = jnp.float32
BF16 = jnp.bfloat16

RMS_EPS = 1e-6
GLA_HEADS = 4
GLA_CHUNK = 64
GLA_GATE_RANK = 16
GLA_GATE_NORMALIZER = 16.0
SB_HEADS = 16
LANES = 128
MXU_TILE = 256
LOG2E = 1.4426950408889634
VMEM_LIMIT = 48 * 1024 * 1024
SB_SKIP_LOG2 = 160.0
SB_MASKED_Z = -1e30
SB_EXP2_CLAMP = 126.0


def _dot(a, b):
    return jnp.dot(a, b, preferred_element_type=F32)


def _dot_nt(a, b):
    return lax.dot_general(a, b, (((1,), (1,)), ((), ())), preferred_element_type=F32)


def _silu(x):
    return x * jax.nn.sigmoid(x)


def _normed(x):
    return x * lax.rsqrt(jnp.mean(x * x, axis=-1, keepdims=True) + RMS_EPS)


def _split_bf16(x):
    hi = x.astype(BF16)
    return hi, (x - hi.astype(F32)).astype(BF16)


def _params(n_axes):
    return pltpu.CompilerParams(dimension_semantics=("arbitrary",) * n_axes,
                                vmem_limit_bytes=VMEM_LIMIT)


def _ada_kernel(c_ref, w_ref, b_ref, o_ref):
    hi, lo = _split_bf16(_silu(c_ref[...]))
    w = w_ref[...].astype(BF16)
    o_ref[...] = _dot(hi, w) + _dot(lo, w) + b_ref[...]


def _ada(c, w, b, tn=1024):
    n_l, d, n = w.shape
    bsz = c.shape[0]
    return pl.pallas_call(
        _ada_kernel,
        grid=(n_l, n // tn),
        in_specs=[pl.BlockSpec((bsz, d), lambda l, j: (0, 0)),
                  pl.BlockSpec((None, d, tn), lambda l, j: (l, 0, j)),
                  pl.BlockSpec((None, 1, tn), lambda l, j: (l, 0, j))],
        out_specs=pl.BlockSpec((None, bsz, tn), lambda l, j: (l, 0, j)),
        out_shape=jax.ShapeDtypeStruct((n_l, bsz, n), F32),
        compiler_params=_params(2),
        name="ada",
    )(c, w, b.reshape(n_l, 1, n))


def _gla_proj_kernel(x_ref, cond_ref, gain_ref, w_ref, wlr_ref, wgk_ref, bgk_ref,
                     qe_ref, ke_ref, v_ref, g_ref, dec_ref, w_s, *, d, dk, tm, qscale):
    n_main = 2 * dk + 2 * d

    @pl.when(pl.program_id(0) == 0)
    def _():
        w_s[...] = w_ref[0:n_main, :].astype(BF16)

    c = GLA_CHUNK
    half = MXU_TILE
    shift = cond_ref[:, 0:d]
    scale = cond_ref[:, d:2 * d]
    h = (_normed(x_ref[...]) * gain_ref[...] * (1.0 + scale) + shift).astype(BF16)
    lr = _dot_nt(h, wlr_ref[...]).astype(BF16)
    pre = _dot(lr, wgk_ref[...]) + bgk_ref[...]
    q = _dot_nt(h, w_s[0:dk, :])
    k = _dot_nt(h, w_s[dk:2 * dk, :])
    gk = jax.nn.log_sigmoid(pre) * (1.0 / GLA_GATE_NORMALIZER)
    hi, lo = _split_bf16(gk)

    ri = lax.broadcasted_iota(jnp.int32, (half, half), 0)
    ci = lax.broadcasted_iota(jnp.int32, (half, half), 1)
    same_chunk = (ri // c) == (ci // c)
    tril = jnp.where(jnp.logical_and(same_chunk, ci <= ri), 1.0, 0.0).astype(BF16)
    tril2 = jnp.concatenate([tril, tril], axis=1)
    n_c = tm // c
    cj = lax.broadcasted_iota(jnp.int32, (n_c, tm), 1) // c
    cr = lax.broadcasted_iota(jnp.int32, (n_c, tm), 0)
    sel = jnp.where(cj == cr, 1.0, 0.0).astype(BF16)
    b_last = _dot(jnp.concatenate([sel, sel], axis=1), jnp.concatenate([hi, lo], axis=0))
    dec_ref[...] = jnp.exp(b_last)
    bs = [_dot(tril2, jnp.concatenate([hi[s * half:(s + 1) * half], lo[s * half:(s + 1) * half]], axis=0))
          for s in range(tm // half)]
    v_ref[...] = _dot_nt(h, w_s[2 * dk:2 * dk + d, :]).astype(BF16)
    g_ref[...] = _dot_nt(h, w_s[2 * dk + d:2 * dk + 2 * d, :]).astype(BF16)
    for s, b in enumerate(bs):
        rows = slice(s * half, (s + 1) * half)
        qe_ref[rows, :] = (q[rows] * qscale * jnp.exp(b)).astype(BF16)
        ke_ref[rows, :] = (k[rows] * jnp.exp(-b)).astype(BF16)


def _gla_proj(x2, cond, gain, w_in, w_lr, w_gk2, b_gk, seq, tm=512):
    n, d = x2.shape
    dk = w_gk2.shape[1]
    tpb = seq // tm
    n_c = tm // GLA_CHUNK
    row = lambda i: (i, 0)
    fixed = lambda i: (0, 0)
    qscale = (dk // GLA_HEADS) ** -0.5
    return pl.pallas_call(
        functools.partial(_gla_proj_kernel, d=d, dk=dk, tm=tm, qscale=qscale),
        grid=(n // tm,),
        in_specs=[pl.BlockSpec((tm, d), row),
                  pl.BlockSpec((None, 1, 3 * d), lambda i: (i // tpb, 0, 0)),
                  pl.BlockSpec((1, d), fixed),
                  pl.BlockSpec(w_in.shape, fixed, pipeline_mode=pl.Buffered(1)),
                  pl.BlockSpec(w_lr.shape, fixed),
                  pl.BlockSpec(w_gk2.shape, fixed),
                  pl.BlockSpec((1, dk), fixed)],
        out_specs=[pl.BlockSpec((tm, dk), row), pl.BlockSpec((tm, dk), row),
                   pl.BlockSpec((tm, d), row), pl.BlockSpec((tm, d), row),
                   pl.BlockSpec((n_c, dk), row)],
        out_shape=[jax.ShapeDtypeStruct((n, dk), BF16), jax.ShapeDtypeStruct((n, dk), BF16),
                   jax.ShapeDtypeStruct((n, d), BF16), jax.ShapeDtypeStruct((n, d), BF16),
                   jax.ShapeDtypeStruct((n // GLA_CHUNK, dk), F32)],
        scratch_shapes=[pltpu.VMEM((2 * dk + 2 * d, d), BF16)],
        compiler_params=_params(1),
        name="gla_proj",
    )(x2, cond, gain, w_in, w_lr, w_gk2, b_gk)


def _gla_core_kernel(qe_ref, ke_ref, v_ref, dec_ref, g_ref, x_ref, cond_ref, ogain_ref, w_ref,
                     x1_ref, s_ref, w_s, y_s, *, tile, tiles_per_seq, d, hk, hv):
    c = GLA_CHUNK
    n_c = tile // c
    heads = range(GLA_HEADS)
    ri = lax.broadcasted_iota(jnp.int32, (c, c), 0)
    ci = lax.broadcasted_iota(jnp.int32, (c, c), 1)
    causal = ci <= ri

    @pl.when(pl.program_id(0) == 0)
    def _():
        w_s[...] = w_ref[...].astype(BF16)

    @pl.when(pl.program_id(0) % tiles_per_seq == 0)
    def _():
        s_ref[...] = jnp.zeros(s_ref.shape, F32)

    def chunk(ref, h, u, width):
        return ref[u * c:(u + 1) * c, h * width:(h + 1) * width]

    work = [(h, u) for u in range(n_c) for h in heads]
    qs = {hu: chunk(qe_ref, *hu, hk) for hu in work}
    ks = {hu: chunk(ke_ref, *hu, hk) for hu in work}
    vs = {hu: chunk(v_ref, *hu, hv) for hu in work}
    scores = {hu: _dot_nt(qs[hu], ks[hu]) for hu in work}
    masked = {hu: jnp.where(causal, scores[hu], 0.0).astype(BF16) for hu in work}
    dec_t = [dec_ref[:, h * hk:(h + 1) * hk].T for h in heads]
    s = [s_ref[h] for h in heads]
    for u in range(n_c):
        rows = slice(u * c, (u + 1) * c)
        incs = [_dot(ks[h, u].T, vs[h, u]) for h in heads]
        o = []
        for h in heads:
            lhs = jnp.concatenate([qs[h, u], masked[h, u]], axis=1)
            rhs = jnp.concatenate([s[h].astype(BF16), vs[h, u]], axis=0)
            o.append(_dot(lhs, rhs))
            s[h] = (s[h] + incs[h]) * dec_t[h][:, u:u + 1]
        y = jnp.concatenate([_normed(o[h]) * ogain_ref[...] for h in heads], axis=1)
        y_s[rows, :] = (y * _silu(g_ref[rows, :].astype(F32))).astype(BF16)
    for h in heads:
        s_ref[h] = s[h]

    x1_ref[...] = x_ref[...] + cond_ref[:, 2 * d:3 * d] * _dot(y_s[...], w_s[...])


def _gla_core(qe, ke, v, dec, g, x2, cond, ogain, w_out, bsz, seq, tile=512):
    d = x2.shape[-1]
    dk = qe.shape[-1]
    n_c = tile // GLA_CHUNK
    tpb = seq // tile
    row = lambda i: (i, 0)
    fixed = lambda i: (0, 0)
    return pl.pallas_call(
        functools.partial(_gla_core_kernel, tile=tile, tiles_per_seq=tpb, d=d, hk=dk // GLA_HEADS,
                          hv=d // GLA_HEADS),
        grid=(bsz * tpb,),
        in_specs=[pl.BlockSpec((tile, dk), row), pl.BlockSpec((tile, dk), row),
                  pl.BlockSpec((tile, d), row), pl.BlockSpec((n_c, dk), row),
                  pl.BlockSpec((tile, d), row), pl.BlockSpec((tile, d), row),
                  pl.BlockSpec((None, 1, 3 * d), lambda i: (i // tpb, 0, 0)),
                  pl.BlockSpec((1, d // GLA_HEADS), fixed),
                  pl.BlockSpec((d, d), fixed, pipeline_mode=pl.Buffered(1))],
        out_specs=pl.BlockSpec((tile, d), row),
        out_shape=jax.ShapeDtypeStruct((bsz * seq, d), F32),
        scratch_shapes=[pltpu.VMEM((GLA_HEADS, dk // GLA_HEADS, d // GLA_HEADS), F32),
                        pltpu.VMEM((d, d), BF16), pltpu.VMEM((tile, d), BF16)],
        compiler_params=_params(1),
        name="gla_core",
    )(qe, ke, v, dec, g, x2, cond, ogain, w_out)


def _kvq_proj_kernel(x_ref, ckv_ref, c2_ref, kvgain_ref, gain_ref, wkv_ref, win_ref,
                     k_ref, v_ref, q_ref, g_ref, wkv_s, win_s, *, d, qscale):
    @pl.when(pl.program_id(0) == 0)
    def _():
        wkv_s[...] = wkv_ref[...].astype(BF16)
        win_s[...] = win_ref[...].astype(BF16)

    y = _normed(x_ref[...])
    hkv = (y * kvgain_ref[...] * (1.0 + ckv_ref[:, d:2 * d]) + ckv_ref[:, 0:d]).astype(BF16)
    h2 = (y * gain_ref[...] * (1.0 + c2_ref[:, d:2 * d]) + c2_ref[:, 0:d]).astype(BF16)
    k_ref[...] = _dot(hkv, wkv_s[:, 0:d]).astype(BF16)
    v_ref[...] = _dot(hkv, wkv_s[:, d:2 * d]).astype(BF16)
    q_ref[...] = (_dot(h2, win_s[:, 0:d]) * qscale).astype(BF16)
    g_ref[...] = _dot(h2, win_s[:, d:2 * d]).astype(BF16)


def _kvq_proj(x1, ckv, c2, kvgain, gain, w_kv, w_in, seq, qscale, tm=512):
    n, d = x1.shape
    tpb = seq // tm
    row = lambda i: (i, 0)
    fixed = lambda i: (0, 0)
    once = dict(pipeline_mode=pl.Buffered(1))
    return pl.pallas_call(
        functools.partial(_kvq_proj_kernel, d=d, qscale=qscale),
        grid=(n // tm,),
        in_specs=[pl.BlockSpec((tm, d), row),
                  pl.BlockSpec((None, 1, 2 * d), lambda i: (i // tpb, 0, 0)),
                  pl.BlockSpec((None, 1, 3 * d), lambda i: (i // tpb, 0, 0)),
                  pl.BlockSpec((1, d), fixed), pl.BlockSpec((1, d), fixed),
                  pl.BlockSpec((d, 2 * d), fixed, **once), pl.BlockSpec((d, 2 * d), fixed, **once)],
        out_specs=[pl.BlockSpec((tm, d), row)] * 4,
        out_shape=[jax.ShapeDtypeStruct((n, d), BF16)] * 4,
        scratch_shapes=[pltpu.VMEM((d, 2 * d), BF16), pltpu.VMEM((d, 2 * d), BF16)],
        compiler_params=_params(1),
        name="kvq_proj",
    )(x1, ckv, c2, kvgain, gain, w_kv, w_in)


def _sb_attn_kernel(q_ref, k_ref, v_ref, o_ref, *, seq, blk, dh, pairs):
    n_blk = seq // blk
    pw = 2 * dh
    lane = lax.broadcasted_iota(jnp.int32, (blk, pw), 1)
    ri = lax.broadcasted_iota(jnp.int32, (blk, blk), 0)
    ci = lax.broadcasted_iota(jnp.int32, (blk, blk), 1)
    strict = jnp.where(ci < ri, 1.0, 0.0).astype(BF16)
    hb = blk // 2
    below_q = (lax.broadcasted_iota(jnp.int32, (hb, hb), 1)
               < lax.broadcasted_iota(jnp.int32, (hb, hb), 0))

    def rows(j):
        return pl.ds(pl.multiple_of(j * blk, blk), blk)

    def stacked_q(i, p):
        q2 = q_ref[rows(i), p * pw:(p + 1) * pw]
        zero = jnp.zeros_like(q2)
        return jnp.concatenate([jnp.where(lane < dh, q2, zero), jnp.where(lane >= dh, q2, zero)], axis=0)

    def soft(z, masked=False):
        if masked:
            z = jnp.where(below_q, z, SB_MASKED_Z)
        sp = jnp.maximum(jnp.log2(1.0 + jnp.exp2(jnp.minimum(z, SB_EXP2_CLAMP))), z)
        return z - sp, sp

    def quadrants(a, r0):
        return a[r0:r0 + hb, 0:hb], a[r0 + hb:r0 + blk, 0:hb], a[r0 + hb:r0 + blk, hb:blk]

    def assemble(tl, bl, br):
        top = jnp.concatenate([tl, jnp.zeros_like(tl)], axis=1)
        return jnp.concatenate([top, jnp.concatenate([bl, br], axis=1)], axis=0)

    def diag_soft(z):
        lbs, sps = [], []
        for r0 in (0, blk):
            tl, bl, br = quadrants(z, r0)
            (lb_tl, sp_tl), (lb_bl, sp_bl), (lb_br, sp_br) = soft(tl, True), soft(bl), soft(br, True)
            lbs.append(assemble(lb_tl, lb_bl, lb_br))
            sps.append(assemble(sp_tl, sp_bl, sp_br))
        return jnp.concatenate(lbs, axis=0), jnp.concatenate(sps, axis=0)

    def diag_weights(lb, tot):
        ws = []
        for r0 in (0, blk):
            ws.append(assemble(*[jnp.exp2(l - t).astype(BF16)
                                 for l, t in zip(quadrants(lb, r0), quadrants(tot, r0))]))
        return jnp.concatenate(ws, axis=0)

    def run_blocks(probs, js, diag_first, carries):
        work = [(p, n) for n in range(len(js)) for p in range(len(probs))]
        zs = {(p, n): _dot_nt(probs[p][1], k_ref[rows(js[n]), probs[p][0] * pw:(probs[p][0] + 1) * pw])
              for p, n in work}
        lbs, sps = {}, {}
        for key in work:
            lbs[key], sps[key] = diag_soft(zs[key]) if diag_first and key[1] == 0 else soft(zs[key])
        sums = {key: _dot(sps[key].astype(BF16), strict) for key in work}
        carries = list(carries)
        ws = {}
        for p, n in work:
            tot = sums[p, n]
            if carries[p] is not None:
                tot = tot + jnp.concatenate([carries[p]] * (blk // LANES), axis=1)
            if diag_first and n == 0:
                ws[p, n] = diag_weights(lbs[p, n], tot)
            else:
                ws[p, n] = jnp.exp2(lbs[p, n] - tot).astype(BF16)
            carries[p] = jnp.broadcast_to(tot[:, 0:1] + sps[p, n][:, 0:1], (2 * blk, LANES))
        accs = []
        for p in range(len(probs)):
            lanes = slice(probs[p][0] * pw, (probs[p][0] + 1) * pw)
            acc = _dot(ws[p, 0], v_ref[rows(js[0]), lanes])
            for n in range(1, len(js)):
                acc = acc + _dot(ws[p, n], v_ref[rows(js[n]), lanes])
            accs.append(acc)
        return accs, carries

    def finish(i, p, acc):
        o_ref[rows(i), p * pw:(p + 1) * pw] = jnp.where(lane < dh, acc[:blk], acc[blk:]).astype(o_ref.dtype)

    probs0 = [(p, stacked_q(0, p)) for p in range(pairs)]
    accs0, _ = run_blocks(probs0, [0], True, [None] * pairs)
    for p in range(pairs):
        finish(0, p, accs0[p])

    def query_block(i, _):
        probs = [(p, stacked_q(i, p)) for p in range(pairs)]
        accs, carries = run_blocks(probs, [i, i - 1], True, [None] * pairs)
        for p in range(pairs):
            def cond(st):
                j, cmin, _, _ = st
                return jnp.logical_and(j >= 0, cmin <= SB_SKIP_LOG2)

            def body(st, p=p):
                j, _, carry, acc = st
                pv, cs = run_blocks([probs[p]], [j], False, [carry])
                return j - 1, jnp.min(cs[0]), cs[0], acc + pv[0]

            st = lax.while_loop(cond, body, (i - 2, jnp.min(carries[p]), carries[p], accs[p]))
            finish(i, p, st[3])
        return 0

    lax.fori_loop(1, n_blk, query_block, 0)


def _sb_attn(q, k, v, bsz, seq, blk=256, pairs=2):
    d = q.shape[-1]
    dh = d // SB_HEADS
    gw = pairs * 2 * dh
    q3, k3, v3 = (t.reshape(bsz, seq, d) for t in (q, k, v))
    spec = pl.BlockSpec((None, seq, gw), lambda b, p: (b, 0, p))
    return pl.pallas_call(
        functools.partial(_sb_attn_kernel, seq=seq, blk=blk, dh=dh, pairs=pairs),
        grid=(bsz, d // gw),
        in_specs=[spec, spec, spec],
        out_specs=spec,
        out_shape=jax.ShapeDtypeStruct((bsz, seq, d), BF16),
        compiler_params=_params(2),
        name="sb_attn",
    )(q3, k3, v3)


def _sb_out_kernel(o_ref, g_ref, x_ref, cond_ref, fgain_ref, w_ref, out_ref, w_s, *, d):
    @pl.when(pl.program_id(0) == 0)
    def _():
        w_s[...] = w_ref[...].astype(BF16)

    gate = cond_ref[:, 2 * d:3 * d]
    y = (o_ref[...].astype(F32) * _silu(g_ref[...].astype(F32))).astype(BF16)
    x2 = x_ref[...] + gate * _dot(y, w_s[...])
    out_ref[...] = _normed(x2) * fgain_ref[...]


def _sb_out(o2, g, x1, cond, fgain, w_out, seq, tm=1024):
    n, d = x1.shape
    tpb = seq // tm
    row = lambda i: (i, 0)
    fixed = lambda i: (0, 0)
    return pl.pallas_call(
        functools.partial(_sb_out_kernel, d=d),
        grid=(n // tm,),
        in_specs=[pl.BlockSpec((tm, d), row), pl.BlockSpec((tm, d), row), pl.BlockSpec((tm, d), row),
                  pl.BlockSpec((None, 1, 3 * d), lambda i: (i // tpb, 0, 0)),
                  pl.BlockSpec((1, d), fixed),
                  pl.BlockSpec((d, d), fixed, pipeline_mode=pl.Buffered(1))],
        out_specs=pl.BlockSpec((tm, d), row),
        out_shape=jax.ShapeDtypeStruct((n, d), F32),
        scratch_shapes=[pltpu.VMEM((d, d), BF16)],
        compiler_params=_params(1),
        name="sb_out",
    )(o2, g, x1, cond, fgain, w_out)


def kernel(x, c, norm_gain, w_ada, b_ada, gla_w_in, gla_w_gk2, gla_b_gk, gla_o_gain, gla_w_out,
           kv_gain, kv_w_ada, kv_b_ada, w_kv, sb_w_in, sb_w_out, final_gain):
    bsz, seq, d = x.shape
    assert w_ada.shape[0] == 2 and gla_w_in.shape[0] == 1 and sb_w_in.shape[0] == 1
    dk = gla_w_gk2.shape[-1]
    n = bsz * seq
    x2 = x.reshape(n, d)

    cond = _ada(c, w_ada, b_ada)
    cond_kv = _ada(c, kv_w_ada[None], kv_b_ada[None])[0]
    cond1 = cond[0].reshape(bsz, 1, 3 * d)
    cond2 = cond[1].reshape(bsz, 1, 3 * d)
    cond_kv = cond_kv.reshape(bsz, 1, 2 * d)

    w_in = jnp.swapaxes(gla_w_in[0], 0, 1)
    n_main = 2 * dk + 2 * d
    w_lr = jnp.pad(w_in[n_main:], ((0, LANES - GLA_GATE_RANK), (0, 0))).astype(BF16)
    w_gk2 = jnp.pad(gla_w_gk2[0], ((0, LANES - GLA_GATE_RANK), (0, 0))).astype(BF16)
    qe, ke, v, g, dec = _gla_proj(x2, cond1, norm_gain[0:1], w_in, w_lr, w_gk2,
                                  gla_b_gk[0:1], seq)
    x1 = _gla_core(qe, ke, v, dec, g, x2, cond1, gla_o_gain[0:1], gla_w_out[0], bsz, seq)

    dh = d // SB_HEADS
    qscale = (dh ** -0.5) * LOG2E
    ks, vs, qs, g2 = _kvq_proj(x1, cond_kv, cond2, kv_gain.reshape(1, d), norm_gain[1:2],
                               w_kv, sb_w_in[0], seq, qscale)
    o2 = _sb_attn(qs, ks, vs, bsz, seq)
    out = _sb_out(o2.reshape(n, d), g2, x1, cond2, final_gain.reshape(1, d),
                  sb_w_out[0], seq)
    return out.reshape(bsz, seq, d)
```

```python
import functools

import jax
import jax.numpy as jnp
from jax import lax
from jax.experimental import pallas as pl
from jax.experimental.pallas import tpu as pltpu

F32 = jnp.float32
BF16 = jnp.bfloat16

RMS_EPS = 1e-6
GLA_HEADS = 4
GLA_CHUNK = 64
GLA_GATE_RANK = 16
GLA_GATE_NORMALIZER = 16.0
SB_HEADS = 16
LANES = 128
MXU_TILE = 256
LOG2E = 1.4426950408889634
VMEM_LIMIT = 48 * 1024 * 1024
SB_SKIP_LOG2 = 160.0
SB_MASKED_Z = -1e30
SB_EXP2_CLAMP = 126.0


def _dot(a, b):
    return jnp.dot(a, b, preferred_element_type=F32)


def _dot_nt(a, b):
    return lax.dot_general(a, b, (((1,), (1,)), ((), ())), preferred_element_type=F32)


def _silu(x):
    return x * jax.nn.sigmoid(x)


def _normed(x):
    return x * lax.rsqrt(jnp.mean(x * x, axis=-1, keepdims=True) + RMS_EPS)


def _split_bf16(x):
    hi = x.astype(BF16)
    return hi, (x - hi.astype(F32)).astype(BF16)


def _params(n_axes):
    return pltpu.CompilerParams(dimension_semantics=("arbitrary",) * n_axes,
                                vmem_limit_bytes=VMEM_LIMIT)


def _ada_kernel(c_ref, w_ref, b_ref, o_ref):
    hi, lo = _split_bf16(_silu(c_ref[...]))
    w = w_ref[...].astype(BF16)
    o_ref[...] = _dot(hi, w) + _dot(lo, w) + b_ref[...]


def _ada(c, w, b, tn=1024):
    n_l, d, n = w.shape
    bsz = c.shape[0]
    return pl.pallas_call(
        _ada_kernel,
        grid=(n_l, n // tn),
        in_specs=[pl.BlockSpec((bsz, d), lambda l, j: (0, 0)),
                  pl.BlockSpec((None, d, tn), lambda l, j: (l, 0, j)),
                  pl.BlockSpec((None, 1, tn), lambda l, j: (l, 0, j))],
        out_specs=pl.BlockSpec((None, bsz, tn), lambda l, j: (l, 0, j)),
        out_shape=jax.ShapeDtypeStruct((n_l, bsz, n), F32),
        compiler_params=_params(2),
        name="ada",
    )(c, w, b.reshape(n_l, 1, n))


def _gla_proj_kernel(x_ref, cond_ref, gain_ref, w_ref, wlr_ref, wgk_ref, bgk_ref,
                     qe_ref, ke_ref, v_ref, g_ref, dec_ref, w_s, *, d, dk, tm, qscale):
    n_main = 2 * dk + 2 * d

    @pl.when(pl.program_id(0) == 0)
    def _():
        w_s[...] = w_ref[0:n_main, :].astype(BF16)

    c = GLA_CHUNK
    half = MXU_TILE
    shift = cond_ref[:, 0:d]
    scale = cond_ref[:, d:2 * d]
    h = (_normed(x_ref[...]) * gain_ref[...] * (1.0 + scale) + shift).astype(BF16)
    lr = jnp.concatenate([_dot_nt(h[:tm // 2], wlr_ref[...]), _dot_nt(h[tm // 2:], wlr_ref[...])],
                         axis=0).astype(BF16)
    pre = _dot(lr, wgk_ref[...]) + bgk_ref[...]
    q = _dot_nt(h, w_s[0:dk, :])
    k = _dot_nt(h, w_s[dk:2 * dk, :])
    gk = jax.nn.log_sigmoid(pre) * (1.0 / GLA_GATE_NORMALIZER)
    hi, lo = _split_bf16(gk)

    ri = lax.broadcasted_iota(jnp.int32, (half, half), 0)
    ci = lax.broadcasted_iota(jnp.int32, (half, half), 1)
    same_chunk = (ri // c) == (ci // c)
    tril = jnp.where(jnp.logical_and(same_chunk, ci <= ri), 1.0, 0.0).astype(BF16)
    tril2 = jnp.concatenate([tril, tril], axis=1)
    n_c = tm // c
    cj = lax.broadcasted_iota(jnp.int32, (n_c, tm), 1) // c
    cr = lax.broadcasted_iota(jnp.int32, (n_c, tm), 0)
    sel = jnp.where(cj == cr, 1.0, 0.0).astype(BF16)
    b_last = _dot(jnp.concatenate([sel, sel], axis=1), jnp.concatenate([hi, lo], axis=0))
    dec_ref[...] = jnp.exp(b_last)
    bs = [_dot(tril2, jnp.concatenate([hi[s * half:(s + 1) * half], lo[s * half:(s + 1) * half]], axis=0))
          for s in range(tm // half)]
    v_ref[...] = _dot_nt(h, w_s[2 * dk:2 * dk + d, :]).astype(BF16)
    g_ref[...] = _dot_nt(h, w_s[2 * dk + d:2 * dk + 2 * d, :]).astype(BF16)
    for s, b in enumerate(bs):
        rows = slice(s * half, (s + 1) * half)
        qe_ref[rows, :] = (q[rows] * qscale * jnp.exp(b)).astype(BF16)
        ke_ref[rows, :] = (k[rows] * jnp.exp(-b)).astype(BF16)


def _gla_proj(x2, cond, gain, w_in, w_lr, w_gk2, b_gk, seq, tm=512):
    n, d = x2.shape
    dk = w_gk2.shape[1]
    tpb = seq // tm
    n_c = tm // GLA_CHUNK
    row = lambda i: (i, 0)
    fixed = lambda i: (0, 0)
    qscale = (dk // GLA_HEADS) ** -0.5
    return pl.pallas_call(
        functools.partial(_gla_proj_kernel, d=d, dk=dk, tm=tm, qscale=qscale),
        grid=(n // tm,),
        in_specs=[pl.BlockSpec((tm, d), row),
                  pl.BlockSpec((None, 1, 3 * d), lambda i: (i // tpb, 0, 0)),
                  pl.BlockSpec((1, d), fixed),
                  pl.BlockSpec(w_in.shape, fixed, pipeline_mode=pl.Buffered(1)),
                  pl.BlockSpec(w_lr.shape, fixed),
                  pl.BlockSpec(w_gk2.shape, fixed),
                  pl.BlockSpec((1, dk), fixed)],
        out_specs=[pl.BlockSpec((tm, dk), row), pl.BlockSpec((tm, dk), row),
                   pl.BlockSpec((tm, d), row), pl.BlockSpec((tm, d), row),
                   pl.BlockSpec((n_c, dk), row)],
        out_shape=[jax.ShapeDtypeStruct((n, dk), BF16), jax.ShapeDtypeStruct((n, dk), BF16),
                   jax.ShapeDtypeStruct((n, d), BF16), jax.ShapeDtypeStruct((n, d), BF16),
                   jax.ShapeDtypeStruct((n // GLA_CHUNK, dk), F32)],
        scratch_shapes=[pltpu.VMEM((2 * dk + 2 * d, d), BF16)],
        compiler_params=_params(1),
        name="gla_proj",
    )(x2, cond, gain, w_in, w_lr, w_gk2, b_gk)


def _gla_core_kernel(qe_ref, ke_ref, v_ref, dec_ref, g_ref, x_ref, cond_ref, ogain_ref, w_ref,
                     x1_ref, s_ref, w_s, y_s, *, tile, tiles_per_seq, d, hk, hv):
    c = GLA_CHUNK
    n_c = tile // c
    heads = range(GLA_HEADS)
    ri = lax.broadcasted_iota(jnp.int32, (c, c), 0)
    ci = lax.broadcasted_iota(jnp.int32, (c, c), 1)
    causal = ci <= ri

    @pl.when(pl.program_id(0) == 0)
    def _():
        w_s[...] = w_ref[...].astype(BF16)

    @pl.when(pl.program_id(0) % tiles_per_seq == 0)
    def _():
        s_ref[...] = jnp.zeros(s_ref.shape, F32)

    def chunk(ref, h, u, width):
        return ref[u * c:(u + 1) * c, h * width:(h + 1) * width]

    work = [(h, u) for u in range(n_c) for h in heads]
    qs = {hu: chunk(qe_ref, *hu, hk) for hu in work}
    ks = {hu: chunk(ke_ref, *hu, hk) for hu in work}
    vs = {hu: chunk(v_ref, *hu, hv) for hu in work}
    scores = {hu: _dot_nt(qs[hu], ks[hu]) for hu in work}
    masked = {hu: jnp.where(causal, scores[hu], 0.0).astype(BF16) for hu in work}
    dec_t = [dec_ref[:, h * hk:(h + 1) * hk].T for h in heads]
    s = [s_ref[h] for h in heads]
    for u in range(n_c):
        rows = slice(u * c, (u + 1) * c)
        incs = [_dot(ks[h, u].T, vs[h, u]) for h in heads]
        o = []
        for h in heads:
            lhs = jnp.concatenate([qs[h, u], masked[h, u]], axis=1)
            rhs = jnp.concatenate([s[h].astype(BF16), vs[h, u]], axis=0)
            o.append(_dot(lhs, rhs))
            s[h] = (s[h] + incs[h]) * dec_t[h][:, u:u + 1]
        y = jnp.concatenate([_normed(o[h]) * ogain_ref[...] for h in heads], axis=1)
        y_s[rows, :] = (y * _silu(g_ref[rows, :].astype(F32))).astype(BF16)
    for h in heads:
        s_ref[h] = s[h]

    x1_ref[...] = x_ref[...] + cond_ref[:, 2 * d:3 * d] * _dot(y_s[...], w_s[...])


def _gla_core(qe, ke, v, dec, g, x2, cond, ogain, w_out, bsz, seq, tile=1024):
    d = x2.shape[-1]
    dk = qe.shape[-1]
    n_c = tile // GLA_CHUNK
    tpb = seq // tile
    row = lambda i: (i, 0)
    fixed = lambda i: (0, 0)
    return pl.pallas_call(
        functools.partial(_gla_core_kernel, tile=tile, tiles_per_seq=tpb, d=d, hk=dk // GLA_HEADS,
                          hv=d // GLA_HEADS),
        grid=(bsz * tpb,),
        in_specs=[pl.BlockSpec((tile, dk), row), pl.BlockSpec((tile, dk), row),
                  pl.BlockSpec((tile, d), row), pl.BlockSpec((n_c, dk), row),
                  pl.BlockSpec((tile, d), row), pl.BlockSpec((tile, d), row),
                  pl.BlockSpec((None, 1, 3 * d), lambda i: (i // tpb, 0, 0)),
                  pl.BlockSpec((1, d // GLA_HEADS), fixed),
                  pl.BlockSpec((d, d), fixed, pipeline_mode=pl.Buffered(1))],
        out_specs=pl.BlockSpec((tile, d), row),
        out_shape=jax.ShapeDtypeStruct((bsz * seq, d), F32),
        scratch_shapes=[pltpu.VMEM((GLA_HEADS, dk // GLA_HEADS, d // GLA_HEADS), F32),
                        pltpu.VMEM((d, d), BF16), pltpu.VMEM((tile, d), BF16)],
        compiler_params=_params(1),
        name="gla_core",
    )(qe, ke, v, dec, g, x2, cond, ogain, w_out)


def _kvq_proj_kernel(x_ref, ckv_ref, c2_ref, kvgain_ref, gain_ref, wkv_ref, win_ref,
                     k_ref, v_ref, q_ref, g_ref, wkv_s, win_s, *, d, qscale):
    @pl.when(pl.program_id(0) == 0)
    def _():
        wkv_s[...] = wkv_ref[...].astype(BF16)
        win_s[...] = win_ref[...].astype(BF16)

    y = _normed(x_ref[...])
    hkv = (y * kvgain_ref[...] * (1.0 + ckv_ref[:, d:2 * d]) + ckv_ref[:, 0:d]).astype(BF16)
    h2 = (y * gain_ref[...] * (1.0 + c2_ref[:, d:2 * d]) + c2_ref[:, 0:d]).astype(BF16)
    k_ref[...] = _dot(hkv, wkv_s[:, 0:d]).astype(BF16)
    v_ref[...] = _dot(hkv, wkv_s[:, d:2 * d]).astype(BF16)
    q_ref[...] = (_dot(h2, win_s[:, 0:d]) * qscale).astype(BF16)
    g_ref[...] = _dot(h2, win_s[:, d:2 * d]).astype(BF16)


def _kvq_proj(x1, ckv, c2, kvgain, gain, w_kv, w_in, seq, qscale, tm=512):
    n, d = x1.shape
    tpb = seq // tm
    row = lambda i: (i, 0)
    fixed = lambda i: (0, 0)
    once = dict(pipeline_mode=pl.Buffered(1))
    return pl.pallas_call(
        functools.partial(_kvq_proj_kernel, d=d, qscale=qscale),
        grid=(n // tm,),
        in_specs=[pl.BlockSpec((tm, d), row),
                  pl.BlockSpec((None, 1, 2 * d), lambda i: (i // tpb, 0, 0)),
                  pl.BlockSpec((None, 1, 3 * d), lambda i: (i // tpb, 0, 0)),
                  pl.BlockSpec((1, d), fixed), pl.BlockSpec((1, d), fixed),
                  pl.BlockSpec((d, 2 * d), fixed, **once), pl.BlockSpec((d, 2 * d), fixed, **once)],
        out_specs=[pl.BlockSpec((tm, d), row)] * 4,
        out_shape=[jax.ShapeDtypeStruct((n, d), BF16)] * 4,
        scratch_shapes=[pltpu.VMEM((d, 2 * d), BF16), pltpu.VMEM((d, 2 * d), BF16)],
        compiler_params=_params(1),
        name="kvq_proj",
    )(x1, ckv, c2, kvgain, gain, w_kv, w_in)


def _sb_attn_kernel(q_ref, k_ref, v_ref, o_ref, *, seq, blk, dh, pairs):
    n_blk = seq // blk
    pw = 2 * dh
    lane = lax.broadcasted_iota(jnp.int32, (blk, pw), 1)
    ri = lax.broadcasted_iota(jnp.int32, (blk, blk), 0)
    ci = lax.broadcasted_iota(jnp.int32, (blk, blk), 1)
    strict = jnp.where(ci < ri, 1.0, 0.0).astype(BF16)
    hb = blk // 2
    below_q = (lax.broadcasted_iota(jnp.int32, (hb, hb), 1)
               < lax.broadcasted_iota(jnp.int32, (hb, hb), 0))

    def rows(j):
        return pl.ds(pl.multiple_of(j * blk, blk), blk)

    def stacked_q(i, p):
        q2 = q_ref[rows(i), p * pw:(p + 1) * pw]
        zero = jnp.zeros_like(q2)
        return jnp.concatenate([jnp.where(lane < dh, q2, zero), jnp.where(lane >= dh, q2, zero)], axis=0)

    def soft(z, masked=False):
        if masked:
            z = jnp.where(below_q, z, SB_MASKED_Z)
        sp = jnp.maximum(jnp.log2(1.0 + jnp.exp2(jnp.minimum(z, SB_EXP2_CLAMP))), z)
        return z - sp, sp

    def quadrants(a, r0):
        return a[r0:r0 + hb, 0:hb], a[r0 + hb:r0 + blk, 0:hb], a[r0 + hb:r0 + blk, hb:blk]

    def assemble(tl, bl, br):
        top = jnp.concatenate([tl, jnp.zeros_like(tl)], axis=1)
        return jnp.concatenate([top, jnp.concatenate([bl, br], axis=1)], axis=0)

    def diag_soft(z):
        lbs, sps = [], []
        for r0 in (0, blk):
            tl, bl, br = quadrants(z, r0)
            (lb_tl, sp_tl), (lb_bl, sp_bl), (lb_br, sp_br) = soft(tl, True), soft(bl), soft(br, True)
            lbs.append(assemble(lb_tl, lb_bl, lb_br))
            sps.append(assemble(sp_tl, sp_bl, sp_br))
        return jnp.concatenate(lbs, axis=0), jnp.concatenate(sps, axis=0)

    def diag_weights(lb, tot):
        ws = []
        for r0 in (0, blk):
            ws.append(assemble(*[jnp.exp2(l - t).astype(BF16)
                                 for l, t in zip(quadrants(lb, r0), quadrants(tot, r0))]))
        return jnp.concatenate(ws, axis=0)

    def run_blocks(probs, js, diag_first, carries):
        work = [(p, n) for n in range(len(js)) for p in range(len(probs))]
        zs = {(p, n): _dot_nt(probs[p][1], k_ref[rows(js[n]), probs[p][0] * pw:(probs[p][0] + 1) * pw])
              for p, n in work}
        lbs, sps = {}, {}
        for key in work:
            lbs[key], sps[key] = diag_soft(zs[key]) if diag_first and key[1] == 0 else soft(zs[key])
        sums = {key: _dot(sps[key].astype(BF16), strict) for key in work}
        carries = list(carries)
        ws = {}
        for p, n in work:
            tot = sums[p, n]
            if carries[p] is not None:
                tot = tot + jnp.concatenate([carries[p]] * (blk // LANES), axis=1)
            if diag_first and n == 0:
                ws[p, n] = diag_weights(lbs[p, n], tot)
            else:
                ws[p, n] = jnp.exp2(lbs[p, n] - tot).astype(BF16)
            carries[p] = jnp.broadcast_to(tot[:, 0:1] + sps[p, n][:, 0:1], (2 * blk, LANES))
        accs = []
        for p in range(len(probs)):
            lanes = slice(probs[p][0] * pw, (probs[p][0] + 1) * pw)
            acc = _dot(ws[p, 0], v_ref[rows(js[0]), lanes])
            for n in range(1, len(js)):
                acc = acc + _dot(ws[p, n], v_ref[rows(js[n]), lanes])
            accs.append(acc)
        return accs, carries

    def finish(i, p, acc):
        o_ref[rows(i), p * pw:(p + 1) * pw] = jnp.where(lane < dh, acc[:blk], acc[blk:]).astype(o_ref.dtype)

    probs0 = [(p, stacked_q(0, p)) for p in range(pairs)]
    accs0, _ = run_blocks(probs0, [0], True, [None] * pairs)
    for p in range(pairs):
        finish(0, p, accs0[p])

    def query_block(i, _):
        probs = [(p, stacked_q(i, p)) for p in range(pairs)]
        accs, carries = run_blocks(probs, [i, i - 1], True, [None] * pairs)
        for p in range(pairs):
            def cond(st):
                j, cmin, _, _ = st
                return jnp.logical_and(j >= 0, cmin <= SB_SKIP_LOG2)

            def body(st, p=p):
                j, _, carry, acc = st
                pv, cs = run_blocks([probs[p]], [j], False, [carry])
                return j - 1, jnp.min(cs[0]), cs[0], acc + pv[0]

            st = lax.while_loop(cond, body, (i - 2, jnp.min(carries[p]), carries[p], accs[p]))
            finish(i, p, st[3])
        return 0

    lax.fori_loop(1, n_blk, query_block, 0)


def _sb_attn(q, k, v, bsz, seq, blk=256, pairs=2):
    d = q.shape[-1]
    dh = d // SB_HEADS
    gw = pairs * 2 * dh
    q3, k3, v3 = (t.reshape(bsz, seq, d) for t in (q, k, v))
    spec = pl.BlockSpec((None, seq, gw), lambda b, p: (b, 0, p))
    return pl.pallas_call(
        functools.partial(_sb_attn_kernel, seq=seq, blk=blk, dh=dh, pairs=pairs),
        grid=(bsz, d // gw),
        in_specs=[spec, spec, spec],
        out_specs=spec,
        out_shape=jax.ShapeDtypeStruct((bsz, seq, d), BF16),
        compiler_params=_params(2),
        name="sb_attn",
    )(q3, k3, v3)


def _sb_out_kernel(o_ref, g_ref, x_ref, cond_ref, fgain_ref, w_ref, out_ref, w_s, *, d):
    @pl.when(pl.program_id(0) == 0)
    def _():
        w_s[...] = w_ref[...].astype(BF16)

    gate = cond_ref[:, 2 * d:3 * d]
    y = (o_ref[...].astype(F32) * _silu(g_ref[...].astype(F32))).astype(BF16)
    x2 = x_ref[...] + gate * _dot(y, w_s[...])
    out_ref[...] = _normed(x2) * fgain_ref[...]


def _sb_out(o2, g, x1, cond, fgain, w_out, seq, tm=1024):
    n, d = x1.shape
    tpb = seq // tm
    row = lambda i: (i, 0)
    fixed = lambda i: (0, 0)
    return pl.pallas_call(
        functools.partial(_sb_out_kernel, d=d),
        grid=(n // tm,),
        in_specs=[pl.BlockSpec((tm, d), row), pl.BlockSpec((tm, d), row), pl.BlockSpec((tm, d), row),
                  pl.BlockSpec((None, 1, 3 * d), lambda i: (i // tpb, 0, 0)),
                  pl.BlockSpec((1, d), fixed),
                  pl.BlockSpec((d, d), fixed, pipeline_mode=pl.Buffered(1))],
        out_specs=pl.BlockSpec((tm, d), row),
        out_shape=jax.ShapeDtypeStruct((n, d), F32),
        scratch_shapes=[pltpu.VMEM((d, d), BF16)],
        compiler_params=_params(1),
        name="sb_out",
    )(o2, g, x1, cond, fgain, w_out)


def kernel(x, c, norm_gain, w_ada, b_ada, gla_w_in, gla_w_gk2, gla_b_gk, gla_o_gain, gla_w_out,
           kv_gain, kv_w_ada, kv_b_ada, w_kv, sb_w_in, sb_w_out, final_gain):
    bsz, seq, d = x.shape
    assert w_ada.shape[0] == 2 and gla_w_in.shape[0] == 1 and sb_w_in.shape[0] == 1
    dk = gla_w_gk2.shape[-1]
    n = bsz * seq
    x2 = x.reshape(n, d)

    cond = _ada(c, w_ada, b_ada)
    cond_kv = _ada(c, kv_w_ada[None], kv_b_ada[None])[0]
    cond1 = cond[0].reshape(bsz, 1, 3 * d)
    cond2 = cond[1].reshape(bsz, 1, 3 * d)
    cond_kv = cond_kv.reshape(bsz, 1, 2 * d)

    w_in = jnp.swapaxes(gla_w_in[0], 0, 1)
    n_main = 2 * dk + 2 * d
    w_lr = jnp.pad(w_in[n_main:], ((0, LANES - GLA_GATE_RANK), (0, 0))).astype(BF16)
    w_gk2 = jnp.pad(gla_w_gk2[0], ((0, LANES - GLA_GATE_RANK), (0, 0))).astype(BF16)
    qe, ke, v, g, dec = _gla_proj(x2, cond1, norm_gain[0:1], w_in, w_lr, w_gk2,
                                  gla_b_gk[0:1], seq)
    x1 = _gla_core(qe, ke, v, dec, g, x2, cond1, gla_o_gain[0:1], gla_w_out[0], bsz, seq)

    dh = d // SB_HEADS
    qscale = (dh ** -0.5) * LOG2E
    ks, vs, qs, g2 = _kvq_proj(x1, cond_kv, cond2, kv_gain.reshape(1, d), norm_gain[1:2],
                               w_kv, sb_w_in[0], seq, qscale)
    o2 = _sb_attn(qs, ks, vs, bsz, seq)
    out = _sb_out(o2.reshape(n, d), g2, x1, cond2, final_gain.reshape(1, d),
                  sb_w_out[0], seq)
    return out.reshape(bsz, seq, d)
```

```python
import functools

import jax
import jax.numpy as jnp
from jax import lax
from jax.experimental import pallas as pl
from jax.experimental.pallas import tpu as pltpu

F32 = jnp.float32
BF16 = jnp.bfloat16

RMS_EPS = 1e-6
GLA_HEADS = 4
GLA_CHUNK = 64
GLA_GATE_RANK = 16
GLA_GATE_NORMALIZER = 16.0
SB_HEADS = 16
LANES = 128
MXU_TILE = 256
LOG2E = 1.4426950408889634
VMEM_LIMIT = 48 * 1024 * 1024
SB_SKIP_LOG2 = 160.0
SB_MASKED_Z = -1e30
SB_EXP2_CLAMP = 126.0


def _dot(a, b):
    return jnp.dot(a, b, preferred_element_type=F32)


def _dot_nt(a, b):
    return lax.dot_general(a, b, (((1,), (1,)), ((), ())), preferred_element_type=F32)


def _silu(x):
    return x * jax.nn.sigmoid(x)


def _normed(x):
    return x * lax.rsqrt(jnp.mean(x * x, axis=-1, keepdims=True) + RMS_EPS)


def _split_bf16(x):
    hi = x.astype(BF16)
    return hi, (x - hi.astype(F32)).astype(BF16)


def _params(n_axes):
    return pltpu.CompilerParams(dimension_semantics=("arbitrary",) * n_axes,
                                vmem_limit_bytes=VMEM_LIMIT)


def _ada_kernel(c_ref, w_ref, b_ref, o_ref):
    hi, lo = _split_bf16(_silu(c_ref[...]))
    w = w_ref[...].astype(BF16)
    o_ref[...] = _dot(hi, w) + _dot(lo, w) + b_ref[...]


def _ada(c, w, b, tn=1024):
    n_l, d, n = w.shape
    bsz = c.shape[0]
    return pl.pallas_call(
        _ada_kernel,
        grid=(n_l, n // tn),
        in_specs=[pl.BlockSpec((bsz, d), lambda l, j: (0, 0)),
                  pl.BlockSpec((None, d, tn), lambda l, j: (l, 0, j)),
                  pl.BlockSpec((None, 1, tn), lambda l, j: (l, 0, j))],
        out_specs=pl.BlockSpec((None, bsz, tn), lambda l, j: (l, 0, j)),
        out_shape=jax.ShapeDtypeStruct((n_l, bsz, n), F32),
        compiler_params=_params(2),
        name="ada",
    )(c, w, b.reshape(n_l, 1, n))


def _gla_proj_kernel(x_ref, cond_ref, gain_ref, w_ref, wlr_ref, wgk_ref, bgk_ref,
                     qe_ref, ke_ref, v_ref, g_ref, dec_ref, w_s, *, d, dk, tm, qscale):
    n_main = 2 * dk + 2 * d

    @pl.when(pl.program_id(0) == 0)
    def _():
        w_s[...] = w_ref[0:n_main, :].astype(BF16)

    c = GLA_CHUNK
    half = MXU_TILE
    shift = cond_ref[:, 0:d]
    scale = cond_ref[:, d:2 * d]
    h = (_normed(x_ref[...]) * gain_ref[...] * (1.0 + scale) + shift).astype(BF16)
    lr = jnp.concatenate([_dot_nt(h[:tm // 2], wlr_ref[...]), _dot_nt(h[tm // 2:], wlr_ref[...])],
                         axis=0).astype(BF16)
    pre = _dot(lr, wgk_ref[...]) + bgk_ref[...]
    q = _dot_nt(h, w_s[0:dk, :])
    k = _dot_nt(h, w_s[dk:2 * dk, :])
    gk = jax.nn.log_sigmoid(pre) * (1.0 / GLA_GATE_NORMALIZER)
    hi, lo = _split_bf16(gk)

    ri = lax.broadcasted_iota(jnp.int32, (half, half), 0)
    ci = lax.broadcasted_iota(jnp.int32, (half, half), 1)
    same_chunk = (ri // c) == (ci // c)
    tril = jnp.where(jnp.logical_and(same_chunk, ci <= ri), 1.0, 0.0).astype(BF16)
    tril2 = jnp.concatenate([tril, tril], axis=1)
    n_c = tm // c
    cj = lax.broadcasted_iota(jnp.int32, (n_c, tm), 1) // c
    cr = lax.broadcasted_iota(jnp.int32, (n_c, tm), 0)
    sel = jnp.where(cj == cr, 1.0, 0.0).astype(BF16)
    b_last = _dot(jnp.concatenate([sel, sel], axis=1), jnp.concatenate([hi, lo], axis=0))
    dec_ref[...] = jnp.exp(b_last)
    bs = [_dot(tril2, jnp.concatenate([hi[s * half:(s + 1) * half], lo[s * half:(s + 1) * half]], axis=0))
          for s in range(tm // half)]
    v_ref[...] = _dot_nt(h, w_s[2 * dk:2 * dk + d, :]).astype(BF16)
    g_ref[...] = _dot_nt(h, w_s[2 * dk + d:2 * dk + 2 * d, :]).astype(BF16)
    for s, b in enumerate(bs):
        rows = slice(s * half, (s + 1) * half)
        qe_ref[rows, :] = (q[rows] * qscale * jnp.exp(b)).astype(BF16)
        ke_ref[rows, :] = (k[rows] * jnp.exp(-b)).astype(BF16)


def _gla_proj(x2, cond, gain, w_in, w_lr, w_gk2, b_gk, seq, tm=512):
    n, d = x2.shape
    dk = w_gk2.shape[1]
    tpb = seq // tm
    n_c = tm // GLA_CHUNK
    row = lambda i: (i, 0)
    fixed = lambda i: (0, 0)
    qscale = (dk // GLA_HEADS) ** -0.5
    return pl.pallas_call(
        functools.partial(_gla_proj_kernel, d=d, dk=dk, tm=tm, qscale=qscale),
        grid=(n // tm,),
        in_specs=[pl.BlockSpec((tm, d), row),
                  pl.BlockSpec((None, 1, 3 * d), lambda i: (i // tpb, 0, 0)),
                  pl.BlockSpec((1, d), fixed),
                  pl.BlockSpec(w_in.shape, fixed, pipeline_mode=pl.Buffered(1)),
                  pl.BlockSpec(w_lr.shape, fixed),
                  pl.BlockSpec(w_gk2.shape, fixed),
                  pl.BlockSpec((1, dk), fixed)],
        out_specs=[pl.BlockSpec((tm, dk), row), pl.BlockSpec((tm, dk), row),
                   pl.BlockSpec((tm, d), row), pl.BlockSpec((tm, d), row),
                   pl.BlockSpec((n_c, dk), row)],
        out_shape=[jax.ShapeDtypeStruct((n, dk), BF16), jax.ShapeDtypeStruct((n, dk), BF16),
                   jax.ShapeDtypeStruct((n, d), BF16), jax.ShapeDtypeStruct((n, d), BF16),
                   jax.ShapeDtypeStruct((n // GLA_CHUNK, dk), F32)],
        scratch_shapes=[pltpu.VMEM((2 * dk + 2 * d, d), BF16)],
        compiler_params=_params(1),
        name="gla_proj",
    )(x2, cond, gain, w_in, w_lr, w_gk2, b_gk)


def _gla_core_kernel(qe_ref, ke_ref, v_ref, dec_ref, g_ref, x_ref, cond_ref, ogain_ref, w_ref,
                     x1_ref, s_ref, w_s, y_s, *, tile, tiles_per_seq, d, hk, hv):
    c = GLA_CHUNK
    n_c = tile // c
    heads = range(GLA_HEADS)
    ri = lax.broadcasted_iota(jnp.int32, (c, c), 0)
    ci = lax.broadcasted_iota(jnp.int32, (c, c), 1)
    causal = ci <= ri

    @pl.when(pl.program_id(0) == 0)
    def _():
        w_s[...] = w_ref[...].astype(BF16)

    @pl.when(pl.program_id(0) % tiles_per_seq == 0)
    def _():
        s_ref[...] = jnp.zeros(s_ref.shape, F32)

    def chunk(ref, h, u, width):
        return ref[u * c:(u + 1) * c, h * width:(h + 1) * width]

    work = [(h, u) for u in range(n_c) for h in heads]
    qs = {hu: chunk(qe_ref, *hu, hk) for hu in work}
    ks = {hu: chunk(ke_ref, *hu, hk) for hu in work}
    vs = {hu: chunk(v_ref, *hu, hv) for hu in work}
    scores = {hu: _dot_nt(qs[hu], ks[hu]) for hu in work}
    masked = {hu: jnp.where(causal, scores[hu], 0.0).astype(BF16) for hu in work}
    dec_t = [dec_ref[:, h * hk:(h + 1) * hk].T for h in heads]
    s = [s_ref[h] for h in heads]
    for u in range(n_c):
        rows = slice(u * c, (u + 1) * c)
        incs = [_dot(ks[h, u].T, vs[h, u]) for h in heads]
        o = []
        for h in heads:
            lhs = jnp.concatenate([qs[h, u], masked[h, u]], axis=1)
            rhs = jnp.concatenate([s[h].astype(BF16), vs[h, u]], axis=0)
            o.append(_dot(lhs, rhs))
            s[h] = (s[h] + incs[h]) * dec_t[h][:, u:u + 1]
        y = jnp.concatenate([_normed(o[h]) * ogain_ref[...] for h in heads], axis=1)
        y_s[rows, :] = (y * _silu(g_ref[rows, :].astype(F32))).astype(BF16)
    for h in heads:
        s_ref[h] = s[h]

    x1_ref[...] = x_ref[...] + cond_ref[:, 2 * d:3 * d] * _dot(y_s[...], w_s[...])


def _gla_core(qe, ke, v, dec, g, x2, cond, ogain, w_out, bsz, seq, tile=1024):
    d = x2.shape[-1]
    dk = qe.shape[-1]
    n_c = tile // GLA_CHUNK
    tpb = seq // tile
    row = lambda i: (i, 0)
    fixed = lambda i: (0, 0)
    return pl.pallas_call(
        functools.partial(_gla_core_kernel, tile=tile, tiles_per_seq=tpb, d=d, hk=dk // GLA_HEADS,
                          hv=d // GLA_HEADS),
        grid=(bsz * tpb,),
        in_specs=[pl.BlockSpec((tile, dk), row), pl.BlockSpec((tile, dk), row),
                  pl.BlockSpec((tile, d), row), pl.BlockSpec((n_c, dk), row),
                  pl.BlockSpec((tile, d), row), pl.BlockSpec((tile, d), row),
                  pl.BlockSpec((None, 1, 3 * d), lambda i: (i // tpb, 0, 0)),
                  pl.BlockSpec((1, d // GLA_HEADS), fixed),
                  pl.BlockSpec((d, d), fixed, pipeline_mode=pl.Buffered(1))],
        out_specs=pl.BlockSpec((tile, d), row),
        out_shape=jax.ShapeDtypeStruct((bsz * seq, d), F32),
        scratch_shapes=[pltpu.VMEM((GLA_HEADS, dk // GLA_HEADS, d // GLA_HEADS), F32),
                        pltpu.VMEM((d, d), BF16), pltpu.VMEM((tile, d), BF16)],
        compiler_params=_params(1),
        name="gla_core",
    )(qe, ke, v, dec, g, x2, cond, ogain, w_out)


def _kvq_proj_kernel(x_ref, ckv_ref, c2_ref, kvgain_ref, gain_ref, wkv_ref, win_ref,
                     k_ref, v_ref, q_ref, g_ref, wkv_s, win_s, *, d, qscale):
    @pl.when(pl.program_id(0) == 0)
    def _():
        wkv_s[...] = wkv_ref[...].astype(BF16)
        win_s[...] = win_ref[...].astype(BF16)

    y = _normed(x_ref[...])
    hkv = (y * kvgain_ref[...] * (1.0 + ckv_ref[:, d:2 * d]) + ckv_ref[:, 0:d]).astype(BF16)
    h2 = (y * gain_ref[...] * (1.0 + c2_ref[:, d:2 * d]) + c2_ref[:, 0:d]).astype(BF16)
    k_ref[...] = _dot(hkv, wkv_s[:, 0:d]).astype(BF16)
    v_ref[...] = _dot(hkv, wkv_s[:, d:2 * d]).astype(BF16)
    q_ref[...] = (_dot(h2, win_s[:, 0:d]) * qscale).astype(BF16)
    g_ref[...] = _dot(h2, win_s[:, d:2 * d]).astype(BF16)


def _kvq_proj(x1, ckv, c2, kvgain, gain, w_kv, w_in, seq, qscale, tm=512):
    n, d = x1.shape
    tpb = seq // tm
    row = lambda i: (i, 0)
    fixed = lambda i: (0, 0)
    once = dict(pipeline_mode=pl.Buffered(1))
    return pl.pallas_call(
        functools.partial(_kvq_proj_kernel, d=d, qscale=qscale),
        grid=(n // tm,),
        in_specs=[pl.BlockSpec((tm, d), row),
                  pl.BlockSpec((None, 1, 2 * d), lambda i: (i // tpb, 0, 0)),
                  pl.BlockSpec((None, 1, 3 * d), lambda i: (i // tpb, 0, 0)),
                  pl.BlockSpec((1, d), fixed), pl.BlockSpec((1, d), fixed),
                  pl.BlockSpec((d, 2 * d), fixed, **once), pl.BlockSpec((d, 2 * d), fixed, **once)],
        out_specs=[pl.BlockSpec((tm, d), row)] * 4,
        out_shape=[jax.ShapeDtypeStruct((n, d), BF16)] * 4,
        scratch_shapes=[pltpu.VMEM((d, 2 * d), BF16), pltpu.VMEM((d, 2 * d), BF16)],
        compiler_params=_params(1),
        name="kvq_proj",
    )(x1, ckv, c2, kvgain, gain, w_kv, w_in)


def _sb_attn_kernel(q_ref, k_ref, v_ref, o_ref, cmin_s, *, seq, blk, dh, pairs):
    n_blk = seq // blk
    pw = 2 * dh
    lane = lax.broadcasted_iota(jnp.int32, (blk, pw), 1)
    ri = lax.broadcasted_iota(jnp.int32, (blk, blk), 0)
    ci = lax.broadcasted_iota(jnp.int32, (blk, blk), 1)
    strict = jnp.where(ci < ri, 1.0, 0.0).astype(BF16)
    hb = blk // 2
    below_q = (lax.broadcasted_iota(jnp.int32, (hb, hb), 1)
               < lax.broadcasted_iota(jnp.int32, (hb, hb), 0))

    def rows(j):
        return pl.ds(pl.multiple_of(j * blk, blk), blk)

    def stacked_q(i, p):
        q2 = q_ref[rows(i), p * pw:(p + 1) * pw]
        zero = jnp.zeros_like(q2)
        return jnp.concatenate([jnp.where(lane < dh, q2, zero), jnp.where(lane >= dh, q2, zero)], axis=0)

    def soft(z, masked=False):
        if masked:
            z = jnp.where(below_q, z, SB_MASKED_Z)
        sp = jnp.maximum(jnp.log2(1.0 + jnp.exp2(jnp.minimum(z, SB_EXP2_CLAMP))), z)
        return z - sp, sp

    def quadrants(a, r0):
        return a[r0:r0 + hb, 0:hb], a[r0 + hb:r0 + blk, 0:hb], a[r0 + hb:r0 + blk, hb:blk]

    def assemble(tl, bl, br):
        top = jnp.concatenate([tl, jnp.zeros_like(tl)], axis=1)
        return jnp.concatenate([top, jnp.concatenate([bl, br], axis=1)], axis=0)

    def diag_soft(z):
        lbs, sps = [], []
        for r0 in (0, blk):
            tl, bl, br = quadrants(z, r0)
            (lb_tl, sp_tl), (lb_bl, sp_bl), (lb_br, sp_br) = soft(tl, True), soft(bl), soft(br, True)
            lbs.append(assemble(lb_tl, lb_bl, lb_br))
            sps.append(assemble(sp_tl, sp_bl, sp_br))
        return jnp.concatenate(lbs, axis=0), jnp.concatenate(sps, axis=0)

    def diag_weights(lb, tot):
        ws = []
        for r0 in (0, blk):
            ws.append(assemble(*[jnp.exp2(l - t).astype(BF16)
                                 for l, t in zip(quadrants(lb, r0), quadrants(tot, r0))]))
        return jnp.concatenate(ws, axis=0)

    def run_blocks(probs, js, diag_first, carries):
        work = [(p, n) for n in range(len(js)) for p in range(len(probs))]
        zs = {(p, n): _dot_nt(probs[p][1], k_ref[rows(js[n]), probs[p][0] * pw:(probs[p][0] + 1) * pw])
              for p, n in work}
        lbs, sps = {}, {}
        for key in work:
            lbs[key], sps[key] = diag_soft(zs[key]) if diag_first and key[1] == 0 else soft(zs[key])
        sums = {key: _dot(sps[key].astype(BF16), strict) for key in work}
        carries = list(carries)
        ws = {}
        for p, n in work:
            tot = sums[p, n]
            if carries[p] is not None:
                tot = tot + jnp.concatenate([carries[p]] * (blk // LANES), axis=1)
            if diag_first and n == 0:
                ws[p, n] = diag_weights(lbs[p, n], tot)
            else:
                ws[p, n] = jnp.exp2(lbs[p, n] - tot).astype(BF16)
            carries[p] = jnp.broadcast_to(tot[:, 0:1] + sps[p, n][:, 0:1], (2 * blk, LANES))
        accs = []
        for p in range(len(probs)):
            lanes = slice(probs[p][0] * pw, (probs[p][0] + 1) * pw)
            acc = _dot(ws[p, 0], v_ref[rows(js[0]), lanes])
            for n in range(1, len(js)):
                acc = acc + _dot(ws[p, n], v_ref[rows(js[n]), lanes])
            accs.append(acc)
        return accs, carries

    def finish(i, p, acc):
        o_ref[rows(i), p * pw:(p + 1) * pw] = jnp.where(lane < dh, acc[:blk], acc[blk:]).astype(o_ref.dtype)

    def qk_stage(i):
        probs = [(p, stacked_q(i, p)) for p in range(pairs)]
        n_keys = 1 if i == 0 else 2
        return {(p, n): _dot_nt(probs[p][1], k_ref[rows(i - n), p * pw:(p + 1) * pw])
                for n in range(n_keys) for p in range(pairs)}

    zs = qk_stage(0)
    for i in range(n_blk):
        work = sorted(zs, key=lambda pn: (pn[1], pn[0]))
        lbs, sps = {}, {}
        for key in work:
            lbs[key], sps[key] = diag_soft(zs[key]) if key[1] == 0 else soft(zs[key])
        sums = {key: _dot(sps[key].astype(BF16), strict) for key in work}
        if i + 1 < n_blk:
            zs = qk_stage(i + 1)
        carries = [None] * pairs
        ws = {}
        for p, n in work:
            tot = sums[p, n]
            if carries[p] is not None:
                tot = tot + jnp.concatenate([carries[p]] * (blk // LANES), axis=1)
            ws[p, n] = diag_weights(lbs[p, n], tot) if n == 0 else jnp.exp2(lbs[p, n] - tot).astype(BF16)
            carries[p] = jnp.broadcast_to(tot[:, 0:1] + sps[p, n][:, 0:1], (2 * blk, LANES))
        for p in range(pairs):
            lanes = slice(p * pw, (p + 1) * pw)
            acc = _dot(ws[p, 0], v_ref[rows(i), lanes])
            if i > 0:
                acc = acc + _dot(ws[p, 1], v_ref[rows(i - 1), lanes])
            finish(i, p, acc)
            cmin_s[i * pairs + p] = jnp.min(carries[p])

    def fix(i, _):
        for p in range(pairs):
            @pl.when(cmin_s[i * pairs + p] <= SB_SKIP_LOG2)
            def _():
                prob = (p, stacked_q(i, p))
                accs, carries = run_blocks([prob], [i, i - 1], True, [None])

                def cond(st):
                    j, cmin, _, _ = st
                    return jnp.logical_and(j >= 0, cmin <= SB_SKIP_LOG2)

                def body(st):
                    j, _, carry, acc = st
                    pv, cs = run_blocks([prob], [j], False, [carry])
                    return j - 1, jnp.min(cs[0]), cs[0], acc + pv[0]

                st = lax.while_loop(cond, body, (i - 2, jnp.min(carries[0]), carries[0], accs[0]))
                finish(i, p, st[3])
        return 0

    lax.fori_loop(2, n_blk, fix, 0)


def _sb_attn(q, k, v, bsz, seq, blk=256, pairs=2):
    d = q.shape[-1]
    dh = d // SB_HEADS
    gw = pairs * 2 * dh
    q3, k3, v3 = (t.reshape(bsz, seq, d) for t in (q, k, v))
    spec = pl.BlockSpec((None, seq, gw), lambda b, p: (b, 0, p))
    return pl.pallas_call(
        functools.partial(_sb_attn_kernel, seq=seq, blk=blk, dh=dh, pairs=pairs),
        grid=(bsz, d // gw),
        in_specs=[spec, spec, spec],
        out_specs=spec,
        out_shape=jax.ShapeDtypeStruct((bsz, seq, d), BF16),
        scratch_shapes=[pltpu.SMEM((seq // blk * pairs,), F32)],
        compiler_params=_params(2),
        name="sb_attn",
    )(q3, k3, v3)


def _sb_out_kernel(o_ref, g_ref, x_ref, cond_ref, fgain_ref, w_ref, out_ref, w_s, *, d):
    @pl.when(pl.program_id(0) == 0)
    def _():
        w_s[...] = w_ref[...].astype(BF16)

    gate = cond_ref[:, 2 * d:3 * d]
    y = (o_ref[...].astype(F32) * _silu(g_ref[...].astype(F32))).astype(BF16)
    x2 = x_ref[...] + gate * _dot(y, w_s[...])
    out_ref[...] = _normed(x2) * fgain_ref[...]


def _sb_out(o2, g, x1, cond, fgain, w_out, seq, tm=1024):
    n, d = x1.shape
    tpb = seq // tm
    row = lambda i: (i, 0)
    fixed = lambda i: (0, 0)
    return pl.pallas_call(
        functools.partial(_sb_out_kernel, d=d),
        grid=(n // tm,),
        in_specs=[pl.BlockSpec((tm, d), row), pl.BlockSpec((tm, d), row), pl.BlockSpec((tm, d), row),
                  pl.BlockSpec((None, 1, 3 * d), lambda i: (i // tpb, 0, 0)),
                  pl.BlockSpec((1, d), fixed),
                  pl.BlockSpec((d, d), fixed, pipeline_mode=pl.Buffered(1))],
        out_specs=pl.BlockSpec((tm, d), row),
        out_shape=jax.ShapeDtypeStruct((n, d), F32),
        scratch_shapes=[pltpu.VMEM((d, d), BF16)],
        compiler_params=_params(1),
        name="sb_out",
    )(o2, g, x1, cond, fgain, w_out)


def kernel(x, c, norm_gain, w_ada, b_ada, gla_w_in, gla_w_gk2, gla_b_gk, gla_o_gain, gla_w_out,
           kv_gain, kv_w_ada, kv_b_ada, w_kv, sb_w_in, sb_w_out, final_gain):
    bsz, seq, d = x.shape
    assert w_ada.shape[0] == 2 and gla_w_in.shape[0] == 1 and sb_w_in.shape[0] == 1
    dk = gla_w_gk2.shape[-1]
    n = bsz * seq
    x2 = x.reshape(n, d)

    cond = _ada(c, w_ada, b_ada)
    cond_kv = _ada(c, kv_w_ada[None], kv_b_ada[None])[0]
    cond1 = cond[0].reshape(bsz, 1, 3 * d)
    cond2 = cond[1].reshape(bsz, 1, 3 * d)
    cond_kv = cond_kv.reshape(bsz, 1, 2 * d)

    w_in = jnp.swapaxes(gla_w_in[0], 0, 1)
    n_main = 2 * dk + 2 * d
    w_lr = jnp.pad(w_in[n_main:], ((0, LANES - GLA_GATE_RANK), (0, 0))).astype(BF16)
    w_gk2 = jnp.pad(gla_w_gk2[0], ((0, LANES - GLA_GATE_RANK), (0, 0))).astype(BF16)
    qe, ke, v, g, dec = _gla_proj(x2, cond1, norm_gain[0:1], w_in, w_lr, w_gk2,
                                  gla_b_gk[0:1], seq)
    x1 = _gla_core(qe, ke, v, dec, g, x2, cond1, gla_o_gain[0:1], gla_w_out[0], bsz, seq)

    dh = d // SB_HEADS
    qscale = (dh ** -0.5) * LOG2E
    ks, vs, qs, g2 = _kvq_proj(x1, cond_kv, cond2, kv_gain.reshape(1, d), norm_gain[1:2],
                               w_kv, sb_w_in[0], seq, qscale)
    o2 = _sb_attn(qs, ks, vs, bsz, seq)
    out = _sb_out(o2.reshape(n, d), g2, x1, cond2, final_gain.reshape(1, d),
                  sb_w_out[0], seq)
    return out.reshape(bsz, seq, d)
```

```python
import functools

import jax
import jax.numpy as jnp
from jax import lax
from jax.experimental import pallas as pl
from jax.experimental.pallas import tpu as pltpu

F32 = jnp.float32
BF16 = jnp.bfloat16

RMS_EPS = 1e-6
GLA_HEADS = 4
GLA_CHUNK = 64
GLA_GATE_RANK = 16
GLA_GATE_NORMALIZER = 16.0
SB_HEADS = 16
LANES = 128
MXU_TILE = 256
LOG2E = 1.4426950408889634
VMEM_LIMIT = 48 * 1024 * 1024
VMEM_LIMIT_LARGE = 56 * 1024 * 1024
SB_SKIP_LOG2 = 160.0
SB_MASKED_Z = -1e30
SB_EXP2_CLAMP = 126.0


def _dot(a, b):
    return jnp.dot(a, b, preferred_element_type=F32)


def _dot_nt(a, b):
    return lax.dot_general(a, b, (((1,), (1,)), ((), ())), preferred_element_type=F32)


def _silu(x):
    return x * jax.nn.sigmoid(x)


def _normed(x):
    return x * lax.rsqrt(jnp.mean(x * x, axis=-1, keepdims=True) + RMS_EPS)


def _split_bf16(x):
    hi = x.astype(BF16)
    return hi, (x - hi.astype(F32)).astype(BF16)


def _params(n_axes, vmem_limit=VMEM_LIMIT):
    return pltpu.CompilerParams(dimension_semantics=("arbitrary",) * n_axes,
                                vmem_limit_bytes=vmem_limit)


def _ada_kernel(c_ref, w_ref, b_ref, o_ref):
    hi, lo = _split_bf16(_silu(c_ref[...]))
    w = w_ref[...].astype(BF16)
    o_ref[...] = _dot(hi, w) + _dot(lo, w) + b_ref[...]


def _ada(c, w, b, tn=1024):
    n_l, d, n = w.shape
    bsz = c.shape[0]
    return pl.pallas_call(
        _ada_kernel,
        grid=(n_l, n // tn),
        in_specs=[pl.BlockSpec((bsz, d), lambda l, j: (0, 0)),
                  pl.BlockSpec((None, d, tn), lambda l, j: (l, 0, j)),
                  pl.BlockSpec((None, 1, tn), lambda l, j: (l, 0, j))],
        out_specs=pl.BlockSpec((None, bsz, tn), lambda l, j: (l, 0, j)),
        out_shape=jax.ShapeDtypeStruct((n_l, bsz, n), F32),
        compiler_params=_params(2),
        name="ada",
    )(c, w, b.reshape(n_l, 1, n))


def _gla_proj_kernel(x_ref, cond_ref, gain_ref, w_ref, wlr_ref, wgk_ref, bgk_ref,
                     qe_ref, ke_ref, v_ref, g_ref, dec_ref, w_s, *, d, dk, tm, qscale):
    n_main = 2 * dk + 2 * d

    @pl.when(pl.program_id(0) == 0)
    def _():
        w_s[...] = w_ref[0:n_main, :].astype(BF16)

    c = GLA_CHUNK
    half = MXU_TILE
    sub = 2 * half
    shift = cond_ref[:, 0:d]
    scale = cond_ref[:, d:2 * d]
    ri = lax.broadcasted_iota(jnp.int32, (half, half), 0)
    ci = lax.broadcasted_iota(jnp.int32, (half, half), 1)
    same_chunk = (ri // c) == (ci // c)
    tril = jnp.where(jnp.logical_and(same_chunk, ci <= ri), 1.0, 0.0).astype(BF16)
    tril2 = jnp.concatenate([tril, tril], axis=1)
    n_c = sub // c
    cj = lax.broadcasted_iota(jnp.int32, (n_c, sub), 1) // c
    cr = lax.broadcasted_iota(jnp.int32, (n_c, sub), 0)
    sel = jnp.where(cj == cr, 1.0, 0.0).astype(BF16)

    for r in range(tm // sub):
        r0 = r * sub
        h = (_normed(x_ref[r0:r0 + sub, :]) * gain_ref[...] * (1.0 + scale) + shift).astype(BF16)
        lr = jnp.concatenate([_dot_nt(h[:half], wlr_ref[...]), _dot_nt(h[half:], wlr_ref[...])],
                             axis=0).astype(BF16)
        pre = _dot(lr, wgk_ref[...]) + bgk_ref[...]
        q = _dot_nt(h, w_s[0:dk, :])
        k = _dot_nt(h, w_s[dk:2 * dk, :])
        gk = jax.nn.log_sigmoid(pre) * (1.0 / GLA_GATE_NORMALIZER)
        hi, lo = _split_bf16(gk)
        b_last = _dot(jnp.concatenate([sel, sel], axis=1), jnp.concatenate([hi, lo], axis=0))
        dec_ref[r * n_c:(r + 1) * n_c, :] = jnp.exp(b_last)
        bs = [_dot(tril2, jnp.concatenate([hi[s * half:(s + 1) * half], lo[s * half:(s + 1) * half]], axis=0))
              for s in range(sub // half)]
        v_ref[r0:r0 + sub, :] = _dot_nt(h, w_s[2 * dk:2 * dk + d, :]).astype(BF16)
        g_ref[r0:r0 + sub, :] = _dot_nt(h, w_s[2 * dk + d:2 * dk + 2 * d, :]).astype(BF16)
        for s, b in enumerate(bs):
            rows = slice(s * half, (s + 1) * half)
            out_rows = slice(r0 + s * half, r0 + (s + 1) * half)
            qe_ref[out_rows, :] = (q[rows] * qscale * jnp.exp(b)).astype(BF16)
            ke_ref[out_rows, :] = (k[rows] * jnp.exp(-b)).astype(BF16)


def _gla_proj(x2, cond, gain, w_in, w_lr, w_gk2, b_gk, seq, tm=1024):
    n, d = x2.shape
    dk = w_gk2.shape[1]
    tpb = seq // tm
    n_c = tm // GLA_CHUNK
    row = lambda i: (i, 0)
    fixed = lambda i: (0, 0)
    qscale = (dk // GLA_HEADS) ** -0.5
    return pl.pallas_call(
        functools.partial(_gla_proj_kernel, d=d, dk=dk, tm=tm, qscale=qscale),
        grid=(n // tm,),
        in_specs=[pl.BlockSpec((tm, d), row),
                  pl.BlockSpec((None, 1, 3 * d), lambda i: (i // tpb, 0, 0)),
                  pl.BlockSpec((1, d), fixed),
                  pl.BlockSpec(w_in.shape, fixed, pipeline_mode=pl.Buffered(1)),
                  pl.BlockSpec(w_lr.shape, fixed),
                  pl.BlockSpec(w_gk2.shape, fixed),
                  pl.BlockSpec((1, dk), fixed)],
        out_specs=[pl.BlockSpec((tm, dk), row), pl.BlockSpec((tm, dk), row),
                   pl.BlockSpec((tm, d), row), pl.BlockSpec((tm, d), row),
                   pl.BlockSpec((n_c, dk), row)],
        out_shape=[jax.ShapeDtypeStruct((n, dk), BF16), jax.ShapeDtypeStruct((n, dk), BF16),
                   jax.ShapeDtypeStruct((n, d), BF16), jax.ShapeDtypeStruct((n, d), BF16),
                   jax.ShapeDtypeStruct((n // GLA_CHUNK, dk), F32)],
        scratch_shapes=[pltpu.VMEM((2 * dk + 2 * d, d), BF16)],
        compiler_params=_params(1),
        name="gla_proj",
    )(x2, cond, gain, w_in, w_lr, w_gk2, b_gk)


def _gla_core_kernel(qe_ref, ke_ref, v_ref, dec_ref, g_ref, x_ref, cond_ref, ogain_ref, w_ref,
                     x1_ref, s_ref, w_s, y_s, *, tile, tiles_per_seq, d, hk, hv):
    c = GLA_CHUNK
    n_c = tile // c
    heads = range(GLA_HEADS)
    ri = lax.broadcasted_iota(jnp.int32, (c, c), 0)
    ci = lax.broadcasted_iota(jnp.int32, (c, c), 1)
    causal = ci <= ri

    @pl.when(pl.program_id(0) == 0)
    def _():
        w_s[...] = w_ref[...].astype(BF16)

    @pl.when(pl.program_id(0) % tiles_per_seq == 0)
    def _():
        s_ref[...] = jnp.zeros(s_ref.shape, F32)

    def chunk(ref, h, u, width):
        return ref[u * c:(u + 1) * c, h * width:(h + 1) * width]

    work = [(h, u) for u in range(n_c) for h in heads]
    qs = {hu: chunk(qe_ref, *hu, hk) for hu in work}
    ks = {hu: chunk(ke_ref, *hu, hk) for hu in work}
    vs = {hu: chunk(v_ref, *hu, hv) for hu in work}
    scores = {hu: _dot_nt(qs[hu], ks[hu]) for hu in work}
    masked = {hu: jnp.where(causal, scores[hu], 0.0).astype(BF16) for hu in work}
    dec_t = [dec_ref[:, h * hk:(h + 1) * hk].T for h in heads]
    s = [s_ref[h] for h in heads]
    for u in range(n_c):
        rows = slice(u * c, (u + 1) * c)
        incs = [_dot(ks[h, u].T, vs[h, u]) for h in heads]
        o = []
        for h in heads:
            lhs = jnp.concatenate([qs[h, u], masked[h, u]], axis=1)
            rhs = jnp.concatenate([s[h].astype(BF16), vs[h, u]], axis=0)
            o.append(_dot(lhs, rhs))
            s[h] = (s[h] + incs[h]) * dec_t[h][:, u:u + 1]
        y = jnp.concatenate([_normed(o[h]) * ogain_ref[...] for h in heads], axis=1)
        y_s[rows, :] = (y * _silu(g_ref[rows, :].astype(F32))).astype(BF16)
    for h in heads:
        s_ref[h] = s[h]

    x1_ref[...] = x_ref[...] + cond_ref[:, 2 * d:3 * d] * _dot(y_s[...], w_s[...])


def _gla_core(qe, ke, v, dec, g, x2, cond, ogain, w_out, bsz, seq, tile=1024):
    d = x2.shape[-1]
    dk = qe.shape[-1]
    n_c = tile // GLA_CHUNK
    tpb = seq // tile
    row = lambda i: (i, 0)
    fixed = lambda i: (0, 0)
    return pl.pallas_call(
        functools.partial(_gla_core_kernel, tile=tile, tiles_per_seq=tpb, d=d, hk=dk // GLA_HEADS,
                          hv=d // GLA_HEADS),
        grid=(bsz * tpb,),
        in_specs=[pl.BlockSpec((tile, dk), row), pl.BlockSpec((tile, dk), row),
                  pl.BlockSpec((tile, d), row), pl.BlockSpec((n_c, dk), row),
                  pl.BlockSpec((tile, d), row), pl.BlockSpec((tile, d), row),
                  pl.BlockSpec((None, 1, 3 * d), lambda i: (i // tpb, 0, 0)),
                  pl.BlockSpec((1, d // GLA_HEADS), fixed),
                  pl.BlockSpec((d, d), fixed, pipeline_mode=pl.Buffered(1))],
        out_specs=pl.BlockSpec((tile, d), row),
        out_shape=jax.ShapeDtypeStruct((bsz * seq, d), F32),
        scratch_shapes=[pltpu.VMEM((GLA_HEADS, dk // GLA_HEADS, d // GLA_HEADS), F32),
                        pltpu.VMEM((d, d), BF16), pltpu.VMEM((tile, d), BF16)],
        compiler_params=_params(1),
        name="gla_core",
    )(qe, ke, v, dec, g, x2, cond, ogain, w_out)


def _kvq_proj_kernel(x_ref, ckv_ref, c2_ref, kvgain_ref, gain_ref, wkv_ref, win_ref,
                     k_ref, v_ref, q_ref, g_ref, wkv_s, win_s, *, d, qscale, sub):
    @pl.when(pl.program_id(0) == 0)
    def _():
        wkv_s[...] = wkv_ref[...].astype(BF16)
        win_s[...] = win_ref[...].astype(BF16)

    for r in range(x_ref.shape[0] // sub):
        rows = slice(r * sub, (r + 1) * sub)
        y = _normed(x_ref[rows, :])
        hkv = (y * kvgain_ref[...] * (1.0 + ckv_ref[:, d:2 * d]) + ckv_ref[:, 0:d]).astype(BF16)
        h2 = (y * gain_ref[...] * (1.0 + c2_ref[:, d:2 * d]) + c2_ref[:, 0:d]).astype(BF16)
        k_ref[rows, :] = _dot(hkv, wkv_s[:, 0:d]).astype(BF16)
        v_ref[rows, :] = _dot(hkv, wkv_s[:, d:2 * d]).astype(BF16)
        q_ref[rows, :] = (_dot(h2, win_s[:, 0:d]) * qscale).astype(BF16)
        g_ref[rows, :] = _dot(h2, win_s[:, d:2 * d]).astype(BF16)


def _kvq_proj(x1, ckv, c2, kvgain, gain, w_kv, w_in, seq, qscale, tm=1024, sub=512):
    n, d = x1.shape
    tpb = seq // tm
    row = lambda i: (i, 0)
    fixed = lambda i: (0, 0)
    once = dict(pipeline_mode=pl.Buffered(1))
    return pl.pallas_call(
        functools.partial(_kvq_proj_kernel, d=d, qscale=qscale, sub=sub),
        grid=(n // tm,),
        in_specs=[pl.BlockSpec((tm, d), row),
                  pl.BlockSpec((None, 1, 2 * d), lambda i: (i // tpb, 0, 0)),
                  pl.BlockSpec((None, 1, 3 * d), lambda i: (i // tpb, 0, 0)),
                  pl.BlockSpec((1, d), fixed), pl.BlockSpec((1, d), fixed),
                  pl.BlockSpec((d, 2 * d), fixed, **once), pl.BlockSpec((d, 2 * d), fixed, **once)],
        out_specs=[pl.BlockSpec((tm, d), row)] * 4,
        out_shape=[jax.ShapeDtypeStruct((n, d), BF16)] * 4,
        scratch_shapes=[pltpu.VMEM((d, 2 * d), BF16), pltpu.VMEM((d, 2 * d), BF16)],
        compiler_params=_params(1, VMEM_LIMIT_LARGE),
        name="kvq_proj",
    )(x1, ckv, c2, kvgain, gain, w_kv, w_in)


def _sb_attn_kernel(q_ref, k_ref, v_ref, o_ref, cmin_s, *, seq, blk, dh, pairs):
    n_blk = seq // blk
    pw = 2 * dh
    lane = lax.broadcasted_iota(jnp.int32, (blk, pw), 1)
    ri = lax.broadcasted_iota(jnp.int32, (blk, blk), 0)
    ci = lax.broadcasted_iota(jnp.int32, (blk, blk), 1)
    strict = jnp.where(ci < ri, 1.0, 0.0).astype(BF16)
    hb = blk // 2
    below_q = (lax.broadcasted_iota(jnp.int32, (hb, hb), 1)
               < lax.broadcasted_iota(jnp.int32, (hb, hb), 0))

    def rows(j):
        return pl.ds(pl.multiple_of(j * blk, blk), blk)

    def stacked_q(i, p):
        q2 = q_ref[rows(i), p * pw:(p + 1) * pw]
        zero = jnp.zeros_like(q2)
        return jnp.concatenate([jnp.where(lane < dh, q2, zero), jnp.where(lane >= dh, q2, zero)], axis=0)

    def soft(z, masked=False):
        if masked:
            z = jnp.where(below_q, z, SB_MASKED_Z)
        sp = jnp.maximum(jnp.log2(1.0 + jnp.exp2(jnp.minimum(z, SB_EXP2_CLAMP))), z)
        return z - sp, sp

    def quadrants(a, r0):
        return a[r0:r0 + hb, 0:hb], a[r0 + hb:r0 + blk, 0:hb], a[r0 + hb:r0 + blk, hb:blk]

    def assemble(tl, bl, br):
        top = jnp.concatenate([tl, jnp.zeros_like(tl)], axis=1)
        return jnp.concatenate([top, jnp.concatenate([bl, br], axis=1)], axis=0)

    def diag_soft(z):
        lbs, sps = [], []
        for r0 in (0, blk):
            tl, bl, br = quadrants(z, r0)
            (lb_tl, sp_tl), (lb_bl, sp_bl), (lb_br, sp_br) = soft(tl, True), soft(bl), soft(br, True)
            lbs.append(assemble(lb_tl, lb_bl, lb_br))
            sps.append(assemble(sp_tl, sp_bl, sp_br))
        return jnp.concatenate(lbs, axis=0), jnp.concatenate(sps, axis=0)

    def diag_weights(lb, tot):
        ws = []
        for r0 in (0, blk):
            ws.append(assemble(*[jnp.exp2(l - t).astype(BF16)
                                 for l, t in zip(quadrants(lb, r0), quadrants(tot, r0))]))
        return jnp.concatenate(ws, axis=0)

    def run_blocks(probs, js, diag_first, carries):
        work = [(p, n) for n in range(len(js)) for p in range(len(probs))]
        zs = {(p, n): _dot_nt(probs[p][1], k_ref[rows(js[n]), probs[p][0] * pw:(probs[p][0] + 1) * pw])
              for p, n in work}
        lbs, sps = {}, {}
        for key in work:
            lbs[key], sps[key] = diag_soft(zs[key]) if diag_first and key[1] == 0 else soft(zs[key])
        sums = {key: _dot(sps[key].astype(BF16), strict) for key in work}
        carries = list(carries)
        ws = {}
        for p, n in work:
            tot = sums[p, n]
            if carries[p] is not None:
                tot = tot + jnp.concatenate([carries[p]] * (blk // LANES), axis=1)
            if diag_first and n == 0:
                ws[p, n] = diag_weights(lbs[p, n], tot)
            else:
                ws[p, n] = jnp.exp2(lbs[p, n] - tot).astype(BF16)
            carries[p] = jnp.broadcast_to(tot[:, 0:1] + sps[p, n][:, 0:1], (2 * blk, LANES))
        accs = []
        for p in range(len(probs)):
            lanes = slice(probs[p][0] * pw, (probs[p][0] + 1) * pw)
            acc = _dot(ws[p, 0], v_ref[rows(js[0]), lanes])
            for n in range(1, len(js)):
                acc = acc + _dot(ws[p, n], v_ref[rows(js[n]), lanes])
            accs.append(acc)
        return accs, carries

    def finish(i, p, acc):
        o_ref[rows(i), p * pw:(p + 1) * pw] = jnp.where(lane < dh, acc[:blk], acc[blk:]).astype(o_ref.dtype)

    def qk_stage(i):
        probs = [(p, stacked_q(i, p)) for p in range(pairs)]
        n_keys = 1 if i == 0 else 2
        return {(p, n): _dot_nt(probs[p][1], k_ref[rows(i - n), p * pw:(p + 1) * pw])
                for n in range(n_keys) for p in range(pairs)}

    zs = qk_stage(0)
    for i in range(n_blk):
        work = sorted(zs, key=lambda pn: (pn[1], pn[0]))
        lbs, sps = {}, {}
        for key in work:
            lbs[key], sps[key] = diag_soft(zs[key]) if key[1] == 0 else soft(zs[key])
        sums = {key: _dot(sps[key].astype(BF16), strict) for key in work}
        if i + 1 < n_blk:
            zs = qk_stage(i + 1)
        carries = [None] * pairs
        ws = {}
        for p, n in work:
            tot = sums[p, n]
            if carries[p] is not None:
                tot = tot + jnp.concatenate([carries[p]] * (blk // LANES), axis=1)
            ws[p, n] = diag_weights(lbs[p, n], tot) if n == 0 else jnp.exp2(lbs[p, n] - tot).astype(BF16)
            carries[p] = jnp.broadcast_to(tot[:, 0:1] + sps[p, n][:, 0:1], (2 * blk, LANES))
        for p in range(pairs):
            lanes = slice(p * pw, (p + 1) * pw)
            acc = _dot(ws[p, 0], v_ref[rows(i), lanes])
            if i > 0:
                acc = acc + _dot(ws[p, 1], v_ref[rows(i - 1), lanes])
            finish(i, p, acc)
            cmin_s[i * pairs + p] = jnp.min(carries[p])

    def fix(i, _):
        for p in range(pairs):
            @pl.when(cmin_s[i * pairs + p] <= SB_SKIP_LOG2)
            def _():
                prob = (p, stacked_q(i, p))
                accs, carries = run_blocks([prob], [i, i - 1], True, [None])

                def cond(st):
                    j, cmin, _, _ = st
                    return jnp.logical_and(j >= 0, cmin <= SB_SKIP_LOG2)

                def body(st):
                    j, _, carry, acc = st
                    pv, cs = run_blocks([prob], [j], False, [carry])
                    return j - 1, jnp.min(cs[0]), cs[0], acc + pv[0]

                st = lax.while_loop(cond, body, (i - 2, jnp.min(carries[0]), carries[0], accs[0]))
                finish(i, p, st[3])
        return 0

    lax.fori_loop(2, n_blk, fix, 0)


def _sb_attn(q, k, v, bsz, seq, blk=256, pairs=2):
    d = q.shape[-1]
    dh = d // SB_HEADS
    gw = pairs * 2 * dh
    q3, k3, v3 = (t.reshape(bsz, seq, d) for t in (q, k, v))
    spec = pl.BlockSpec((None, seq, gw), lambda b, p: (b, 0, p))
    return pl.pallas_call(
        functools.partial(_sb_attn_kernel, seq=seq, blk=blk, dh=dh, pairs=pairs),
        grid=(bsz, d // gw),
        in_specs=[spec, spec, spec],
        out_specs=spec,
        out_shape=jax.ShapeDtypeStruct((bsz, seq, d), BF16),
        scratch_shapes=[pltpu.SMEM((seq // blk * pairs,), F32)],
        compiler_params=_params(2),
        name="sb_attn",
    )(q3, k3, v3)


def _sb_out_kernel(o_ref, g_ref, x_ref, cond_ref, fgain_ref, w_ref, out_ref, w_s, *, d):
    @pl.when(pl.program_id(0) == 0)
    def _():
        w_s[...] = w_ref[...].astype(BF16)

    gate = cond_ref[:, 2 * d:3 * d]
    y = (o_ref[...].astype(F32) * _silu(g_ref[...].astype(F32))).astype(BF16)
    x2 = x_ref[...] + gate * _dot(y, w_s[...])
    out_ref[...] = _normed(x2) * fgain_ref[...]


def _sb_out(o2, g, x1, cond, fgain, w_out, seq, tm=1024):
    n, d = x1.shape
    tpb = seq // tm
    row = lambda i: (i, 0)
    fixed = lambda i: (0, 0)
    return pl.pallas_call(
        functools.partial(_sb_out_kernel, d=d),
        grid=(n // tm,),
        in_specs=[pl.BlockSpec((tm, d), row), pl.BlockSpec((tm, d), row), pl.BlockSpec((tm, d), row),
                  pl.BlockSpec((None, 1, 3 * d), lambda i: (i // tpb, 0, 0)),
                  pl.BlockSpec((1, d), fixed),
                  pl.BlockSpec((d, d), fixed, pipeline_mode=pl.Buffered(1))],
        out_specs=pl.BlockSpec((tm, d), row),
        out_shape=jax.ShapeDtypeStruct((n, d), F32),
        scratch_shapes=[pltpu.VMEM((d, d), BF16)],
        compiler_params=_params(1),
        name="sb_out",
    )(o2, g, x1, cond, fgain, w_out)


def kernel(x, c, norm_gain, w_ada, b_ada, gla_w_in, gla_w_gk2, gla_b_gk, gla_o_gain, gla_w_out,
           kv_gain, kv_w_ada, kv_b_ada, w_kv, sb_w_in, sb_w_out, final_gain):
    bsz, seq, d = x.shape
    assert w_ada.shape[0] == 2 and gla_w_in.shape[0] == 1 and sb_w_in.shape[0] == 1
    dk = gla_w_gk2.shape[-1]
    n = bsz * seq
    x2 = x.reshape(n, d)

    cond = _ada(c, w_ada, b_ada)
    cond_kv = _ada(c, kv_w_ada[None], kv_b_ada[None])[0]
    cond1 = cond[0].reshape(bsz, 1, 3 * d)
    cond2 = cond[1].reshape(bsz, 1, 3 * d)
    cond_kv = cond_kv.reshape(bsz, 1, 2 * d)

    w_in = jnp.swapaxes(gla_w_in[0], 0, 1)
    n_main = 2 * dk + 2 * d
    w_lr = jnp.pad(w_in[n_main:], ((0, LANES - GLA_GATE_RANK), (0, 0))).astype(BF16)
    w_gk2 = jnp.pad(gla_w_gk2[0], ((0, LANES - GLA_GATE_RANK), (0, 0))).astype(BF16)
    qe, ke, v, g, dec = _gla_proj(x2, cond1, norm_gain[0:1], w_in, w_lr, w_gk2,
                                  gla_b_gk[0:1], seq)
    x1 = _gla_core(qe, ke, v, dec, g, x2, cond1, gla_o_gain[0:1], gla_w_out[0], bsz, seq)

    dh = d // SB_HEADS
    qscale = (dh ** -0.5) * LOG2E
    ks, vs, qs, g2 = _kvq_proj(x1, cond_kv, cond2, kv_gain.reshape(1, d), norm_gain[1:2],
                               w_kv, sb_w_in[0], seq, qscale)
    o2 = _sb_attn(qs, ks, vs, bsz, seq)
    out = _sb_out(o2.reshape(n, d), g2, x1, cond2, final_gain.reshape(1, d),
                  sb_w_out[0], seq)
    return out.reshape(bsz, seq, d)
```

```python
import functools

import jax
import jax.numpy as jnp
from jax import lax
from jax.experimental import pallas as pl
from jax.experimental.pallas import tpu as pltpu

F32 = jnp.float32
BF16 = jnp.bfloat16

RMS_EPS = 1e-6
GLA_HEADS = 4
GLA_CHUNK = 64
GLA_GATE_RANK = 16
GLA_GATE_NORMALIZER = 16.0
SB_HEADS = 16
LANES = 128
MXU_TILE = 256
LOG2E = 1.4426950408889634
VMEM_LIMIT = 48 * 1024 * 1024
VMEM_LIMIT_LARGE = 56 * 1024 * 1024
SB_SKIP_LOG2 = 160.0
SB_MASKED_Z = -1e30
SB_EXP2_CLAMP = 126.0


def _dot(a, b):
    return jnp.dot(a, b, preferred_element_type=F32)


def _dot_nt(a, b):
    return lax.dot_general(a, b, (((1,), (1,)), ((), ())), preferred_element_type=F32)


def _silu(x):
    return x * jax.nn.sigmoid(x)


def _normed(x):
    return x * lax.rsqrt(jnp.mean(x * x, axis=-1, keepdims=True) + RMS_EPS)


def _split_bf16(x):
    hi = x.astype(BF16)
    return hi, (x - hi.astype(F32)).astype(BF16)


def _params(n_axes, vmem_limit=VMEM_LIMIT):
    return pltpu.CompilerParams(dimension_semantics=("arbitrary",) * n_axes,
                                vmem_limit_bytes=vmem_limit)


def _ada_kernel(c_ref, w_ref, b_ref, o_ref):
    hi, lo = _split_bf16(_silu(c_ref[...]))
    w = w_ref[...].astype(BF16)
    o_ref[...] = _dot(hi, w) + _dot(lo, w) + b_ref[...]


def _ada(c, w, b, tn=1024):
    n_l, d, n = w.shape
    bsz = c.shape[0]
    return pl.pallas_call(
        _ada_kernel,
        grid=(n_l, n // tn),
        in_specs=[pl.BlockSpec((bsz, d), lambda l, j: (0, 0)),
                  pl.BlockSpec((None, d, tn), lambda l, j: (l, 0, j)),
                  pl.BlockSpec((None, 1, tn), lambda l, j: (l, 0, j))],
        out_specs=pl.BlockSpec((None, bsz, tn), lambda l, j: (l, 0, j)),
        out_shape=jax.ShapeDtypeStruct((n_l, bsz, n), F32),
        compiler_params=_params(2),
        name="ada",
    )(c, w, b.reshape(n_l, 1, n))


def _gla_proj_kernel(x_ref, cond_ref, gain_ref, w_ref, wlr_ref, wgk_ref, bgk_ref,
                     qe_ref, ke_ref, v_ref, g_ref, dec_ref, w_s, *, d, dk, tm, qscale):
    n_main = 2 * dk + 2 * d

    @pl.when(pl.program_id(0) == 0)
    def _():
        w_s[...] = w_ref[0:n_main, :].astype(BF16)

    c = GLA_CHUNK
    half = MXU_TILE
    sub = 2 * half
    shift = cond_ref[:, 0:d]
    scale = cond_ref[:, d:2 * d]
    ri = lax.broadcasted_iota(jnp.int32, (half, half), 0)
    ci = lax.broadcasted_iota(jnp.int32, (half, half), 1)
    same_chunk = (ri // c) == (ci // c)
    tril = jnp.where(jnp.logical_and(same_chunk, ci <= ri), 1.0, 0.0).astype(BF16)
    tril2 = jnp.concatenate([tril, tril], axis=1)
    n_c = sub // c
    cj = lax.broadcasted_iota(jnp.int32, (n_c, sub), 1) // c
    cr = lax.broadcasted_iota(jnp.int32, (n_c, sub), 0)
    sel = jnp.where(cj == cr, 1.0, 0.0).astype(BF16)

    for r in range(tm // sub):
        r0 = r * sub
        h = (_normed(x_ref[r0:r0 + sub, :]) * gain_ref[...] * (1.0 + scale) + shift).astype(BF16)
        lr = jnp.concatenate([_dot_nt(h[:half], wlr_ref[...]), _dot_nt(h[half:], wlr_ref[...])],
                             axis=0).astype(BF16)
        pre = _dot(lr, wgk_ref[...]) + bgk_ref[...]
        q = _dot_nt(h, w_s[0:dk, :])
        k = _dot_nt(h, w_s[dk:2 * dk, :])
        gk = jax.nn.log_sigmoid(pre) * (1.0 / GLA_GATE_NORMALIZER)
        hi, lo = _split_bf16(gk)
        b_last = _dot(jnp.concatenate([sel, sel], axis=1), jnp.concatenate([hi, lo], axis=0))
        dec_ref[r * n_c:(r + 1) * n_c, :] = jnp.exp(b_last)
        bs = [_dot(tril2, jnp.concatenate([hi[s * half:(s + 1) * half], lo[s * half:(s + 1) * half]], axis=0))
              for s in range(sub // half)]
        v_ref[r0:r0 + sub, :] = _dot_nt(h, w_s[2 * dk:2 * dk + d, :]).astype(BF16)
        g_ref[r0:r0 + sub, :] = _dot_nt(h, w_s[2 * dk + d:2 * dk + 2 * d, :]).astype(BF16)
        for s, b in enumerate(bs):
            rows = slice(s * half, (s + 1) * half)
            out_rows = slice(r0 + s * half, r0 + (s + 1) * half)
            qe_ref[out_rows, :] = (q[rows] * qscale * jnp.exp(b)).astype(BF16)
            ke_ref[out_rows, :] = (k[rows] * jnp.exp(-b)).astype(BF16)


def _gla_proj(x2, cond, gain, w_in, w_lr, w_gk2, b_gk, seq, tm=1024):
    n, d = x2.shape
    dk = w_gk2.shape[1]
    assert seq % tm == 0 and tm % (2 * MXU_TILE) == 0 and MXU_TILE % GLA_CHUNK == 0
    tpb = seq // tm
    n_c = tm // GLA_CHUNK
    row = lambda i: (i, 0)
    fixed = lambda i: (0, 0)
    qscale = (dk // GLA_HEADS) ** -0.5
    return pl.pallas_call(
        functools.partial(_gla_proj_kernel, d=d, dk=dk, tm=tm, qscale=qscale),
        grid=(n // tm,),
        in_specs=[pl.BlockSpec((tm, d), row),
                  pl.BlockSpec((None, 1, 3 * d), lambda i: (i // tpb, 0, 0)),
                  pl.BlockSpec((1, d), fixed),
                  pl.BlockSpec(w_in.shape, fixed, pipeline_mode=pl.Buffered(1)),
                  pl.BlockSpec(w_lr.shape, fixed),
                  pl.BlockSpec(w_gk2.shape, fixed),
                  pl.BlockSpec((1, dk), fixed)],
        out_specs=[pl.BlockSpec((tm, dk), row), pl.BlockSpec((tm, dk), row),
                   pl.BlockSpec((tm, d), row), pl.BlockSpec((tm, d), row),
                   pl.BlockSpec((n_c, dk), row)],
        out_shape=[jax.ShapeDtypeStruct((n, dk), BF16), jax.ShapeDtypeStruct((n, dk), BF16),
                   jax.ShapeDtypeStruct((n, d), BF16), jax.ShapeDtypeStruct((n, d), BF16),
                   jax.ShapeDtypeStruct((n // GLA_CHUNK, dk), F32)],
        scratch_shapes=[pltpu.VMEM((2 * dk + 2 * d, d), BF16)],
        compiler_params=_params(1),
        name="gla_proj",
    )(x2, cond, gain, w_in, w_lr, w_gk2, b_gk)


def _gla_core_kernel(qe_ref, ke_ref, v_ref, dec_ref, g_ref, x_ref, cond_ref, ogain_ref, w_ref,
                     x1_ref, s_ref, w_s, y_s, *, tile, tiles_per_seq, d, hk, hv):
    c = GLA_CHUNK
    n_c = tile // c
    heads = range(GLA_HEADS)
    ri = lax.broadcasted_iota(jnp.int32, (c, c), 0)
    ci = lax.broadcasted_iota(jnp.int32, (c, c), 1)
    causal = ci <= ri

    @pl.when(pl.program_id(0) == 0)
    def _():
        w_s[...] = w_ref[...].astype(BF16)

    @pl.when(pl.program_id(0) % tiles_per_seq == 0)
    def _():
        s_ref[...] = jnp.zeros(s_ref.shape, F32)

    def chunk(ref, h, u, width):
        return ref[u * c:(u + 1) * c, h * width:(h + 1) * width]

    work = [(h, u) for u in range(n_c) for h in heads]
    qs = {hu: chunk(qe_ref, *hu, hk) for hu in work}
    ks = {hu: chunk(ke_ref, *hu, hk) for hu in work}
    vs = {hu: chunk(v_ref, *hu, hv) for hu in work}
    scores = {hu: _dot_nt(qs[hu], ks[hu]) for hu in work}
    masked = {hu: jnp.where(causal, scores[hu], 0.0).astype(BF16) for hu in work}
    dec_t = [dec_ref[:, h * hk:(h + 1) * hk].T for h in heads]
    s = [s_ref[h] for h in heads]
    for u in range(n_c):
        rows = slice(u * c, (u + 1) * c)
        incs = [_dot(ks[h, u].T, vs[h, u]) for h in heads]
        o = []
        for h in heads:
            lhs = jnp.concatenate([qs[h, u], masked[h, u]], axis=1)
            rhs = jnp.concatenate([s[h].astype(BF16), vs[h, u]], axis=0)
            o.append(_dot(lhs, rhs))
            s[h] = (s[h] + incs[h]) * dec_t[h][:, u:u + 1]
        y = jnp.concatenate([_normed(o[h]) * ogain_ref[...] for h in heads], axis=1)
        y_s[rows, :] = (y * _silu(g_ref[rows, :].astype(F32))).astype(BF16)
    for h in heads:
        s_ref[h] = s[h]

    x1_ref[...] = x_ref[...] + cond_ref[:, 2 * d:3 * d] * _dot(y_s[...], w_s[...])


def _gla_core(qe, ke, v, dec, g, x2, cond, ogain, w_out, bsz, seq, tile=1024):
    d = x2.shape[-1]
    dk = qe.shape[-1]
    assert seq % tile == 0 and tile % GLA_CHUNK == 0 and (tile // GLA_CHUNK) % 8 == 0
    n_c = tile // GLA_CHUNK
    tpb = seq // tile
    row = lambda i: (i, 0)
    fixed = lambda i: (0, 0)
    return pl.pallas_call(
        functools.partial(_gla_core_kernel, tile=tile, tiles_per_seq=tpb, d=d, hk=dk // GLA_HEADS,
                          hv=d // GLA_HEADS),
        grid=(bsz * tpb,),
        in_specs=[pl.BlockSpec((tile, dk), row), pl.BlockSpec((tile, dk), row),
                  pl.BlockSpec((tile, d), row), pl.BlockSpec((n_c, dk), row),
                  pl.BlockSpec((tile, d), row), pl.BlockSpec((tile, d), row),
                  pl.BlockSpec((None, 1, 3 * d), lambda i: (i // tpb, 0, 0)),
                  pl.BlockSpec((1, d // GLA_HEADS), fixed),
                  pl.BlockSpec((d, d), fixed, pipeline_mode=pl.Buffered(1))],
        out_specs=pl.BlockSpec((tile, d), row),
        out_shape=jax.ShapeDtypeStruct((bsz * seq, d), F32),
        scratch_shapes=[pltpu.VMEM((GLA_HEADS, dk // GLA_HEADS, d // GLA_HEADS), F32),
                        pltpu.VMEM((d, d), BF16), pltpu.VMEM((tile, d), BF16)],
        compiler_params=_params(1),
        name="gla_core",
    )(qe, ke, v, dec, g, x2, cond, ogain, w_out)


def _kvq_proj_kernel(x_ref, ckv_ref, c2_ref, kvgain_ref, gain_ref, wkv_ref, win_ref,
                     k_ref, v_ref, q_ref, g_ref, wkv_s, win_s, *, d, qscale, sub):
    @pl.when(pl.program_id(0) == 0)
    def _():
        wkv_s[...] = wkv_ref[...].astype(BF16)
        win_s[...] = win_ref[...].astype(BF16)

    for r in range(x_ref.shape[0] // sub):
        rows = slice(r * sub, (r + 1) * sub)
        y = _normed(x_ref[rows, :])
        hkv = (y * kvgain_ref[...] * (1.0 + ckv_ref[:, d:2 * d]) + ckv_ref[:, 0:d]).astype(BF16)
        h2 = (y * gain_ref[...] * (1.0 + c2_ref[:, d:2 * d]) + c2_ref[:, 0:d]).astype(BF16)
        k_ref[rows, :] = _dot(hkv, wkv_s[:, 0:d]).astype(BF16)
        v_ref[rows, :] = _dot(hkv, wkv_s[:, d:2 * d]).astype(BF16)
        q_ref[rows, :] = (_dot(h2, win_s[:, 0:d]) * qscale).astype(BF16)
        g_ref[rows, :] = _dot(h2, win_s[:, d:2 * d]).astype(BF16)


def _kvq_proj(x1, ckv, c2, kvgain, gain, w_kv, w_in, seq, qscale, tm=1024, sub=512):
    n, d = x1.shape
    assert seq % tm == 0 and tm % sub == 0
    tpb = seq // tm
    row = lambda i: (i, 0)
    fixed = lambda i: (0, 0)
    once = dict(pipeline_mode=pl.Buffered(1))
    return pl.pallas_call(
        functools.partial(_kvq_proj_kernel, d=d, qscale=qscale, sub=sub),
        grid=(n // tm,),
        in_specs=[pl.BlockSpec((tm, d), row),
                  pl.BlockSpec((None, 1, 2 * d), lambda i: (i // tpb, 0, 0)),
                  pl.BlockSpec((None, 1, 3 * d), lambda i: (i // tpb, 0, 0)),
                  pl.BlockSpec((1, d), fixed), pl.BlockSpec((1, d), fixed),
                  pl.BlockSpec((d, 2 * d), fixed, **once), pl.BlockSpec((d, 2 * d), fixed, **once)],
        out_specs=[pl.BlockSpec((tm, d), row)] * 4,
        out_shape=[jax.ShapeDtypeStruct((n, d), BF16)] * 4,
        scratch_shapes=[pltpu.VMEM((d, 2 * d), BF16), pltpu.VMEM((d, 2 * d), BF16)],
        compiler_params=_params(1, VMEM_LIMIT_LARGE),
        name="kvq_proj",
    )(x1, ckv, c2, kvgain, gain, w_kv, w_in)


def _sb_attn_kernel(q_ref, k_ref, v_ref, o_ref, cmin_s, *, seq, blk, dh, pairs):
    n_blk = seq // blk
    pw = 2 * dh
    lane = lax.broadcasted_iota(jnp.int32, (blk, pw), 1)
    ri = lax.broadcasted_iota(jnp.int32, (blk, blk), 0)
    ci = lax.broadcasted_iota(jnp.int32, (blk, blk), 1)
    strict = jnp.where(ci < ri, 1.0, 0.0).astype(BF16)
    hb = blk // 2
    below_q = (lax.broadcasted_iota(jnp.int32, (hb, hb), 1)
               < lax.broadcasted_iota(jnp.int32, (hb, hb), 0))

    def rows(j):
        return pl.ds(pl.multiple_of(j * blk, blk), blk)

    def stacked_q(i, p):
        q2 = q_ref[rows(i), p * pw:(p + 1) * pw]
        zero = jnp.zeros_like(q2)
        return jnp.concatenate([jnp.where(lane < dh, q2, zero), jnp.where(lane >= dh, q2, zero)], axis=0)

    def soft(z, masked=False):
        if masked:
            z = jnp.where(below_q, z, SB_MASKED_Z)
        sp = jnp.maximum(jnp.log2(1.0 + jnp.exp2(jnp.minimum(z, SB_EXP2_CLAMP))), z)
        return z - sp, sp

    def quadrants(a, r0):
        return a[r0:r0 + hb, 0:hb], a[r0 + hb:r0 + blk, 0:hb], a[r0 + hb:r0 + blk, hb:blk]

    def assemble(tl, bl, br):
        top = jnp.concatenate([tl, jnp.zeros_like(tl)], axis=1)
        return jnp.concatenate([top, jnp.concatenate([bl, br], axis=1)], axis=0)

    def diag_soft(z):
        lbs, sps = [], []
        for r0 in (0, blk):
            tl, bl, br = quadrants(z, r0)
            (lb_tl, sp_tl), (lb_bl, sp_bl), (lb_br, sp_br) = soft(tl, True), soft(bl), soft(br, True)
            lbs.append(assemble(lb_tl, lb_bl, lb_br))
            sps.append(assemble(sp_tl, sp_bl, sp_br))
        return jnp.concatenate(lbs, axis=0), jnp.concatenate(sps, axis=0)

    def diag_weights(lb, tot):
        ws = []
        for r0 in (0, blk):
            ws.append(assemble(*[jnp.exp2(l - t).astype(BF16)
                                 for l, t in zip(quadrants(lb, r0), quadrants(tot, r0))]))
        return jnp.concatenate(ws, axis=0)

    def run_blocks(probs, js, diag_first, carries):
        work = [(p, n) for n in range(len(js)) for p in range(len(probs))]
        zs = {(p, n): _dot_nt(probs[p][1], k_ref[rows(js[n]), probs[p][0] * pw:(probs[p][0] + 1) * pw])
              for p, n in work}
        lbs, sps = {}, {}
        for key in work:
            lbs[key], sps[key] = diag_soft(zs[key]) if diag_first and key[1] == 0 else soft(zs[key])
        sums = {key: _dot(sps[key].astype(BF16), strict) for key in work}
        carries = list(carries)
        ws = {}
        for p, n in work:
            tot = sums[p, n]
            if carries[p] is not None:
                tot = tot + jnp.concatenate([carries[p]] * (blk // LANES), axis=1)
            if diag_first and n == 0:
                ws[p, n] = diag_weights(lbs[p, n], tot)
            else:
                ws[p, n] = jnp.exp2(lbs[p, n] - tot).astype(BF16)
            carries[p] = jnp.broadcast_to(tot[:, 0:1] + sps[p, n][:, 0:1], (2 * blk, LANES))
        accs = []
        for p in range(len(probs)):
            lanes = slice(probs[p][0] * pw, (probs[p][0] + 1) * pw)
            acc = _dot(ws[p, 0], v_ref[rows(js[0]), lanes])
            for n in range(1, len(js)):
                acc = acc + _dot(ws[p, n], v_ref[rows(js[n]), lanes])
            accs.append(acc)
        return accs, carries

    def finish(i, p, acc):
        o_ref[rows(i), p * pw:(p + 1) * pw] = jnp.where(lane < dh, acc[:blk], acc[blk:]).astype(o_ref.dtype)

    def qk_stage(i):
        probs = [(p, stacked_q(i, p)) for p in range(pairs)]
        n_keys = 1 if i == 0 else 2
        return {(p, n): _dot_nt(probs[p][1], k_ref[rows(i - n), p * pw:(p + 1) * pw])
                for n in range(n_keys) for p in range(pairs)}

    zs = qk_stage(0)
    for i in range(n_blk):
        work = sorted(zs, key=lambda pn: (pn[1], pn[0]))
        lbs, sps = {}, {}
        for key in work:
            lbs[key], sps[key] = diag_soft(zs[key]) if key[1] == 0 else soft(zs[key])
        sums = {key: _dot(sps[key].astype(BF16), strict) for key in work}
        if i + 1 < n_blk:
            zs = qk_stage(i + 1)
        carries = [None] * pairs
        ws = {}
        for p, n in work:
            tot = sums[p, n]
            if carries[p] is not None:
                tot = tot + jnp.concatenate([carries[p]] * (blk // LANES), axis=1)
            ws[p, n] = diag_weights(lbs[p, n], tot) if n == 0 else jnp.exp2(lbs[p, n] - tot).astype(BF16)
            carries[p] = jnp.broadcast_to(tot[:, 0:1] + sps[p, n][:, 0:1], (2 * blk, LANES))
        for p in range(pairs):
            lanes = slice(p * pw, (p + 1) * pw)
            acc = _dot(ws[p, 0], v_ref[rows(i), lanes])
            if i > 0:
                acc = acc + _dot(ws[p, 1], v_ref[rows(i - 1), lanes])
            finish(i, p, acc)
            cmin_s[i * pairs + p] = jnp.min(carries[p])

    def fix(i, _):
        for p in range(pairs):
            @pl.when(cmin_s[i * pairs + p] <= SB_SKIP_LOG2)
            def _():
                prob = (p, stacked_q(i, p))
                accs, carries = run_blocks([prob], [i, i - 1], True, [None])

                def cond(st):
                    j, cmin, _, _ = st
                    return jnp.logical_and(j >= 0, cmin <= SB_SKIP_LOG2)

                def body(st):
                    j, _, carry, acc = st
                    pv, cs = run_blocks([prob], [j], False, [carry])
                    return j - 1, jnp.min(cs[0]), cs[0], acc + pv[0]

                st = lax.while_loop(cond, body, (i - 2, jnp.min(carries[0]), carries[0], accs[0]))
                finish(i, p, st[3])
        return 0

    lax.fori_loop(2, n_blk, fix, 0)


def _sb_attn(q, k, v, bsz, seq, blk=256, pairs=1):
    d = q.shape[-1]
    dh = d // SB_HEADS
    gw = pairs * 2 * dh
    assert 2 * dh == LANES and seq % blk == 0 and d % gw == 0
    q3, k3, v3 = (t.reshape(bsz, seq, d) for t in (q, k, v))
    spec = pl.BlockSpec((None, seq, gw), lambda b, p: (b, 0, p))
    return pl.pallas_call(
        functools.partial(_sb_attn_kernel, seq=seq, blk=blk, dh=dh, pairs=pairs),
        grid=(bsz, d // gw),
        in_specs=[spec, spec, spec],
        out_specs=spec,
        out_shape=jax.ShapeDtypeStruct((bsz, seq, d), BF16),
        scratch_shapes=[pltpu.SMEM((seq // blk * pairs,), F32)],
        compiler_params=_params(2),
        name="sb_attn",
    )(q3, k3, v3)


def _sb_out_kernel(o_ref, g_ref, x_ref, cond_ref, fgain_ref, w_ref, out_ref, w_s, *, d):
    @pl.when(pl.program_id(0) == 0)
    def _():
        w_s[...] = w_ref[...].astype(BF16)

    gate = cond_ref[:, 2 * d:3 * d]
    y = (o_ref[...].astype(F32) * _silu(g_ref[...].astype(F32))).astype(BF16)
    x2 = x_ref[...] + gate * _dot(y, w_s[...])
    out_ref[...] = _normed(x2) * fgain_ref[...]


def _sb_out(o2, g, x1, cond, fgain, w_out, seq, tm=1024):
    n, d = x1.shape
    assert seq % tm == 0
    tpb = seq // tm
    row = lambda i: (i, 0)
    fixed = lambda i: (0, 0)
    return pl.pallas_call(
        functools.partial(_sb_out_kernel, d=d),
        grid=(n // tm,),
        in_specs=[pl.BlockSpec((tm, d), row), pl.BlockSpec((tm, d), row), pl.BlockSpec((tm, d), row),
                  pl.BlockSpec((None, 1, 3 * d), lambda i: (i // tpb, 0, 0)),
                  pl.BlockSpec((1, d), fixed),
                  pl.BlockSpec((d, d), fixed, pipeline_mode=pl.Buffered(1))],
        out_specs=pl.BlockSpec((tm, d), row),
        out_shape=jax.ShapeDtypeStruct((n, d), F32),
        scratch_shapes=[pltpu.VMEM((d, d), BF16)],
        compiler_params=_params(1),
        name="sb_out",
    )(o2, g, x1, cond, fgain, w_out)


def kernel(x, c, norm_gain, w_ada, b_ada, gla_w_in, gla_w_gk2, gla_b_gk, gla_o_gain, gla_w_out,
           kv_gain, kv_w_ada, kv_b_ada, w_kv, sb_w_in, sb_w_out, final_gain):
    bsz, seq, d = x.shape
    assert w_ada.shape[0] == 2 and gla_w_in.shape[0] == 1 and sb_w_in.shape[0] == 1
    dk = gla_w_gk2.shape[-1]
    n = bsz * seq
    x2 = x.reshape(n, d)

    cond = _ada(c, w_ada, b_ada)
    cond_kv = _ada(c, kv_w_ada[None], kv_b_ada[None])[0]
    cond1 = cond[0].reshape(bsz, 1, 3 * d)
    cond2 = cond[1].reshape(bsz, 1, 3 * d)
    cond_kv = cond_kv.reshape(bsz, 1, 2 * d)

    w_in = jnp.swapaxes(gla_w_in[0], 0, 1)
    n_main = 2 * dk + 2 * d
    w_lr = jnp.pad(w_in[n_main:], ((0, LANES - GLA_GATE_RANK), (0, 0))).astype(BF16)
    w_gk2 = jnp.pad(gla_w_gk2[0], ((0, LANES - GLA_GATE_RANK), (0, 0))).astype(BF16)
    qe, ke, v, g, dec = _gla_proj(x2, cond1, norm_gain[0:1], w_in, w_lr, w_gk2,
                                  gla_b_gk[0:1], seq)
    x1 = _gla_core(qe, ke, v, dec, g, x2, cond1, gla_o_gain[0:1], gla_w_out[0], bsz, seq)

    dh = d // SB_HEADS
    qscale = (dh ** -0.5) * LOG2E
    ks, vs, qs, g2 = _kvq_proj(x1, cond_kv, cond2, kv_gain.reshape(1, d), norm_gain[1:2],
                               w_kv, sb_w_in[0], seq, qscale)
    o2 = _sb_attn(qs, ks, vs, bsz, seq)
    out = _sb_out(o2.reshape(n, d), g2, x1, cond2, final_gain.reshape(1, d),
                  sb_w_out[0], seq)
    return out.reshape(bsz, seq, d)
```

```python
import functools

import jax
import jax.numpy as jnp
from jax import lax
from jax.experimental import pallas as pl
from jax.experimental.pallas import tpu as pltpu

F32 = jnp.float32
BF16 = jnp.bfloat16

RMS_EPS = 1e-6
GLA_HEADS = 4
GLA_CHUNK = 64
GLA_GATE_RANK = 16
GLA_GATE_NORMALIZER = 16.0
SB_HEADS = 16
LANES = 128
MXU_TILE = 256
LOG2E = 1.4426950408889634
VMEM_LIMIT = 48 * 1024 * 1024
VMEM_LIMIT_LARGE = 56 * 1024 * 1024
SB_SKIP_LOG2 = 160.0
SB_MASKED_Z = -1e30
SB_EXP2_CLAMP = 126.0


def _dot(a, b):
    return jnp.dot(a, b, preferred_element_type=F32)


def _dot_nt(a, b):
    return lax.dot_general(a, b, (((1,), (1,)), ((), ())), preferred_element_type=F32)


def _silu(x):
    return x * jax.nn.sigmoid(x)


def _normed(x):
    return x * lax.rsqrt(jnp.mean(x * x, axis=-1, keepdims=True) + RMS_EPS)


def _split_bf16(x):
    hi = x.astype(BF16)
    return hi, (x - hi.astype(F32)).astype(BF16)


def _params(n_axes, vmem_limit=VMEM_LIMIT):
    return pltpu.CompilerParams(dimension_semantics=("arbitrary",) * n_axes,
                                vmem_limit_bytes=vmem_limit)


def _ada_kernel(c_ref, w_ref, b_ref, o_ref):
    hi, lo = _split_bf16(_silu(c_ref[...]))
    w = w_ref[...].astype(BF16)
    o_ref[...] = _dot(hi, w) + _dot(lo, w) + b_ref[...]


def _ada(c, w, b, tn=1024):
    n_l, d, n = w.shape
    bsz = c.shape[0]
    return pl.pallas_call(
        _ada_kernel,
        grid=(n_l, n // tn),
        in_specs=[pl.BlockSpec((bsz, d), lambda l, j: (0, 0)),
                  pl.BlockSpec((None, d, tn), lambda l, j: (l, 0, j)),
                  pl.BlockSpec((None, 1, tn), lambda l, j: (l, 0, j))],
        out_specs=pl.BlockSpec((None, bsz, tn), lambda l, j: (l, 0, j)),
        out_shape=jax.ShapeDtypeStruct((n_l, bsz, n), F32),
        compiler_params=_params(2),
        name="ada",
    )(c, w, b.reshape(n_l, 1, n))


def _gla_proj_kernel(x_ref, cond_ref, gain_ref, w_ref, wlr_ref, wgk_ref, bgk_ref,
                     qe_ref, ke_ref, v_ref, g_ref, dec_ref, w_s, *, d, dk, tm, qscale):
    n_main = 2 * dk + 2 * d

    @pl.when(pl.program_id(0) == 0)
    def _():
        w_s[...] = w_ref[0:n_main, :].astype(BF16)

    c = GLA_CHUNK
    half = MXU_TILE
    sub = 2 * half
    shift = cond_ref[:, 0:d]
    scale = cond_ref[:, d:2 * d]
    ri = lax.broadcasted_iota(jnp.int32, (half, half), 0)
    ci = lax.broadcasted_iota(jnp.int32, (half, half), 1)
    same_chunk = (ri // c) == (ci // c)
    tril = jnp.where(jnp.logical_and(same_chunk, ci <= ri), 1.0, 0.0).astype(BF16)
    tril2 = jnp.concatenate([tril, tril], axis=1)
    n_c = sub // c
    cj = lax.broadcasted_iota(jnp.int32, (n_c, sub), 1) // c
    cr = lax.broadcasted_iota(jnp.int32, (n_c, sub), 0)
    sel = jnp.where(cj == cr, 1.0, 0.0).astype(BF16)

    for r in range(tm // sub):
        r0 = r * sub
        h = (_normed(x_ref[r0:r0 + sub, :]) * gain_ref[...] * (1.0 + scale) + shift).astype(BF16)
        lr = jnp.concatenate([_dot_nt(h[:half], wlr_ref[...]), _dot_nt(h[half:], wlr_ref[...])],
                             axis=0).astype(BF16)
        pre = _dot(lr, wgk_ref[...]) + bgk_ref[...]
        q = _dot_nt(h, w_s[0:dk, :])
        k = _dot_nt(h, w_s[dk:2 * dk, :])
        gk = jax.nn.log_sigmoid(pre) * (1.0 / GLA_GATE_NORMALIZER)
        hi, lo = _split_bf16(gk)
        b_last = _dot(jnp.concatenate([sel, sel], axis=1), jnp.concatenate([hi, lo], axis=0))
        dec_ref[r * n_c:(r + 1) * n_c, :] = jnp.exp(b_last)
        bs = [_dot(tril2, jnp.concatenate([hi[s * half:(s + 1) * half], lo[s * half:(s + 1) * half]], axis=0))
              for s in range(sub // half)]
        v_ref[r0:r0 + sub, :] = _dot_nt(h, w_s[2 * dk:2 * dk + d, :]).astype(BF16)
        g_ref[r0:r0 + sub, :] = _dot_nt(h, w_s[2 * dk + d:2 * dk + 2 * d, :]).astype(BF16)
        for s, b in enumerate(bs):
            rows = slice(s * half, (s + 1) * half)
            out_rows = slice(r0 + s * half, r0 + (s + 1) * half)
            qe_ref[out_rows, :] = (q[rows] * qscale * jnp.exp(b)).astype(BF16)
            ke_ref[out_rows, :] = (k[rows] * jnp.exp(-b)).astype(BF16)


def _gla_proj(x2, cond, gain, w_in, w_lr, w_gk2, b_gk, seq, tm=1024):
    n, d = x2.shape
    dk = w_gk2.shape[1]
    assert seq % tm == 0 and tm % (2 * MXU_TILE) == 0 and MXU_TILE % GLA_CHUNK == 0
    tpb = seq // tm
    n_c = tm // GLA_CHUNK
    row = lambda i: (i, 0)
    fixed = lambda i: (0, 0)
    qscale = (dk // GLA_HEADS) ** -0.5
    return pl.pallas_call(
        functools.partial(_gla_proj_kernel, d=d, dk=dk, tm=tm, qscale=qscale),
        grid=(n // tm,),
        in_specs=[pl.BlockSpec((tm, d), row),
                  pl.BlockSpec((None, 1, 3 * d), lambda i: (i // tpb, 0, 0)),
                  pl.BlockSpec((1, d), fixed),
                  pl.BlockSpec(w_in.shape, fixed, pipeline_mode=pl.Buffered(1)),
                  pl.BlockSpec(w_lr.shape, fixed),
                  pl.BlockSpec(w_gk2.shape, fixed),
                  pl.BlockSpec((1, dk), fixed)],
        out_specs=[pl.BlockSpec((tm, dk), row), pl.BlockSpec((tm, dk), row),
                   pl.BlockSpec((tm, d), row), pl.BlockSpec((tm, d), row),
                   pl.BlockSpec((n_c, dk), row)],
        out_shape=[jax.ShapeDtypeStruct((n, dk), BF16), jax.ShapeDtypeStruct((n, dk), BF16),
                   jax.ShapeDtypeStruct((n, d), BF16), jax.ShapeDtypeStruct((n, d), BF16),
                   jax.ShapeDtypeStruct((n // GLA_CHUNK, dk), F32)],
        scratch_shapes=[pltpu.VMEM((2 * dk + 2 * d, d), BF16)],
        compiler_params=_params(1),
        name="gla_proj",
    )(x2, cond, gain, w_in, w_lr, w_gk2, b_gk)


def _gla_core_kernel(qe_ref, ke_ref, v_ref, dec_ref, g_ref, x_ref, cond_ref, ogain_ref, w_ref,
                     x1_ref, s_ref, w_s, y_s, *, tile, tiles_per_seq, d, hk, hv):
    c = GLA_CHUNK
    n_c = tile // c
    heads = range(GLA_HEADS)
    ri = lax.broadcasted_iota(jnp.int32, (c, c), 0)
    ci = lax.broadcasted_iota(jnp.int32, (c, c), 1)
    causal = ci <= ri

    @pl.when(pl.program_id(0) == 0)
    def _():
        w_s[...] = w_ref[...].astype(BF16)

    @pl.when(pl.program_id(0) % tiles_per_seq == 0)
    def _():
        s_ref[...] = jnp.zeros(s_ref.shape, F32)

    def chunk(ref, h, u, width):
        return ref[u * c:(u + 1) * c, h * width:(h + 1) * width]

    work = [(h, u) for u in range(n_c) for h in heads]
    qs = {hu: chunk(qe_ref, *hu, hk) for hu in work}
    ks = {hu: chunk(ke_ref, *hu, hk) for hu in work}
    vs = {hu: chunk(v_ref, *hu, hv) for hu in work}
    scores = {hu: _dot_nt(qs[hu], ks[hu]) for hu in work}
    masked = {hu: jnp.where(causal, scores[hu], 0.0).astype(BF16) for hu in work}
    dec_t = [dec_ref[:, h * hk:(h + 1) * hk].T for h in heads]
    s = [s_ref[h] for h in heads]
    for u in range(n_c):
        rows = slice(u * c, (u + 1) * c)
        incs = [_dot(ks[h, u].T, vs[h, u]) for h in heads]
        o = []
        for h in heads:
            lhs = jnp.concatenate([qs[h, u], masked[h, u]], axis=1)
            rhs = jnp.concatenate([s[h].astype(BF16), vs[h, u]], axis=0)
            o.append(_dot(lhs, rhs))
            s[h] = (s[h] + incs[h]) * dec_t[h][:, u:u + 1]
        y = jnp.concatenate([_normed(o[h]) * ogain_ref[...] for h in heads], axis=1)
        y_s[rows, :] = (y * _silu(g_ref[rows, :].astype(F32))).astype(BF16)
    for h in heads:
        s_ref[h] = s[h]

    x1_ref[...] = x_ref[...] + cond_ref[:, 2 * d:3 * d] * _dot(y_s[...], w_s[...])


def _gla_core(qe, ke, v, dec, g, x2, cond, ogain, w_out, bsz, seq, tile=1024):
    d = x2.shape[-1]
    dk = qe.shape[-1]
    assert seq % tile == 0 and tile % GLA_CHUNK == 0 and (tile // GLA_CHUNK) % 8 == 0
    n_c = tile // GLA_CHUNK
    tpb = seq // tile
    row = lambda i: (i, 0)
    fixed = lambda i: (0, 0)
    return pl.pallas_call(
        functools.partial(_gla_core_kernel, tile=tile, tiles_per_seq=tpb, d=d, hk=dk // GLA_HEADS,
                          hv=d // GLA_HEADS),
        grid=(bsz * tpb,),
        in_specs=[pl.BlockSpec((tile, dk), row), pl.BlockSpec((tile, dk), row),
                  pl.BlockSpec((tile, d), row), pl.BlockSpec((n_c, dk), row),
                  pl.BlockSpec((tile, d), row), pl.BlockSpec((tile, d), row),
                  pl.BlockSpec((None, 1, 3 * d), lambda i: (i // tpb, 0, 0)),
                  pl.BlockSpec((1, d // GLA_HEADS), fixed),
                  pl.BlockSpec((d, d), fixed, pipeline_mode=pl.Buffered(1))],
        out_specs=pl.BlockSpec((tile, d), row),
        out_shape=jax.ShapeDtypeStruct((bsz * seq, d), F32),
        scratch_shapes=[pltpu.VMEM((GLA_HEADS, dk // GLA_HEADS, d // GLA_HEADS), F32),
                        pltpu.VMEM((d, d), BF16), pltpu.VMEM((tile, d), BF16)],
        compiler_params=_params(1),
        name="gla_core",
    )(qe, ke, v, dec, g, x2, cond, ogain, w_out)


def _kvq_proj_kernel(x_ref, ckv_ref, c2_ref, kvgain_ref, gain_ref, wkv_ref, win_ref,
                     k_ref, v_ref, q_ref, g_ref, wkv_s, win_s, *, d, qscale, sub):
    @pl.when(pl.program_id(0) == 0)
    def _():
        wkv_s[...] = wkv_ref[...].astype(BF16)
        win_s[...] = win_ref[...].astype(BF16)

    for r in range(x_ref.shape[0] // sub):
        rows = slice(r * sub, (r + 1) * sub)
        y = _normed(x_ref[rows, :])
        hkv = (y * kvgain_ref[...] * (1.0 + ckv_ref[:, d:2 * d]) + ckv_ref[:, 0:d]).astype(BF16)
        h2 = (y * gain_ref[...] * (1.0 + c2_ref[:, d:2 * d]) + c2_ref[:, 0:d]).astype(BF16)
        k_ref[rows, :] = _dot(hkv, wkv_s[:, 0:d]).astype(BF16)
        v_ref[rows, :] = _dot(hkv, wkv_s[:, d:2 * d]).astype(BF16)
        q_ref[rows, :] = (_dot(h2, win_s[:, 0:d]) * qscale).astype(BF16)
        g_ref[rows, :] = _dot(h2, win_s[:, d:2 * d]).astype(BF16)


def _kvq_proj(x1, ckv, c2, kvgain, gain, w_kv, w_in, seq, qscale, tm=1024, sub=512):
    n, d = x1.shape
    assert seq % tm == 0 and tm % sub == 0
    tpb = seq // tm
    row = lambda i: (i, 0)
    fixed = lambda i: (0, 0)
    once = dict(pipeline_mode=pl.Buffered(1))
    return pl.pallas_call(
        functools.partial(_kvq_proj_kernel, d=d, qscale=qscale, sub=sub),
        grid=(n // tm,),
        in_specs=[pl.BlockSpec((tm, d), row),
                  pl.BlockSpec((None, 1, 2 * d), lambda i: (i // tpb, 0, 0)),
                  pl.BlockSpec((None, 1, 3 * d), lambda i: (i // tpb, 0, 0)),
                  pl.BlockSpec((1, d), fixed), pl.BlockSpec((1, d), fixed),
                  pl.BlockSpec((d, 2 * d), fixed, **once), pl.BlockSpec((d, 2 * d), fixed, **once)],
        out_specs=[pl.BlockSpec((tm, d), row)] * 4,
        out_shape=[jax.ShapeDtypeStruct((n, d), BF16)] * 4,
        scratch_shapes=[pltpu.VMEM((d, 2 * d), BF16), pltpu.VMEM((d, 2 * d), BF16)],
        compiler_params=_params(1, VMEM_LIMIT_LARGE),
        name="kvq_proj",
    )(x1, ckv, c2, kvgain, gain, w_kv, w_in)


def _sb_attn_kernel(q_ref, k_ref, v_ref, o_ref, cmin_s, *, seq, blk, dh, pairs, fused):
    n_blk = seq // blk
    pw = 2 * dh
    lane = lax.broadcasted_iota(jnp.int32, (blk, pw), 1)
    ri = lax.broadcasted_iota(jnp.int32, (blk, blk), 0)
    ci = lax.broadcasted_iota(jnp.int32, (blk, blk), 1)
    strict = jnp.where(ci < ri, 1.0, 0.0).astype(BF16)
    ri2 = lax.broadcasted_iota(jnp.int32, (2 * blk, blk), 0)
    ci2 = lax.broadcasted_iota(jnp.int32, (2 * blk, blk), 1)
    below = ci2 < (ri2 & (blk - 1))

    def rows(j):
        return pl.ds(pl.multiple_of(j * blk, blk), blk)

    def lanes(p):
        return slice(p * pw, (p + 1) * pw)

    def stacked_q(i, p):
        q2 = q_ref[rows(i), lanes(p)]
        zero = jnp.zeros_like(q2)
        return jnp.concatenate([jnp.where(lane < dh, q2, zero), jnp.where(lane >= dh, q2, zero)], axis=0)

    def scores(qp, j, p):
        return _dot_nt(qp, k_ref[rows(j), lanes(p)])

    def soft(z, diagonal):
        if diagonal:
            z = jnp.where(below, z, SB_MASKED_Z)
        sp = jnp.maximum(jnp.log2(1.0 + jnp.exp2(jnp.minimum(z, SB_EXP2_CLAMP))), z)
        return z - sp, sp

    def weights(lb, sp, sums, carry):
        tot = sums if carry is None else sums + jnp.concatenate([carry] * (blk // LANES), axis=1)
        w = jnp.exp2(lb - tot).astype(BF16)
        return w, jnp.broadcast_to(tot[:, 0:1] + sp[:, 0:1], (2 * blk, LANES))

    def finish(i, p, acc):
        o_ref[rows(i), lanes(p)] = jnp.where(lane < dh, acc[:blk], acc[blk:]).astype(o_ref.dtype)

    def qk_stage(i):
        qs = [stacked_q(i, p) for p in range(pairs)]
        return {(p, n): scores(qs[p], i - n, p) for n in range(min(i + 1, fused)) for p in range(pairs)}

    zs = qk_stage(0)
    for i in range(n_blk):
        work = sorted(zs, key=lambda pn: (pn[1], pn[0]))
        soft_i = {key: soft(zs[key], key[1] == 0) for key in work}
        sums = {key: _dot(soft_i[key][1].astype(BF16), strict) for key in work}
        if i + 1 < n_blk:
            zs = qk_stage(i + 1)
        carries = [None] * pairs
        ws = {}
        for p, n in work:
            ws[p, n], carries[p] = weights(*soft_i[p, n], sums[p, n], carries[p])
        for p in range(pairs):
            acc = _dot(ws[p, 0], v_ref[rows(i), lanes(p)])
            for n in range(1, min(i + 1, fused)):
                acc = acc + _dot(ws[p, n], v_ref[rows(i - n), lanes(p)])
            finish(i, p, acc)
            cmin_s[i * pairs + p] = jnp.min(carries[p])

    def fix(i, _):
        for p in range(pairs):
            @pl.when(cmin_s[i * pairs + p] <= SB_SKIP_LOG2)
            def _():
                qp = stacked_q(i, p)

                def cond(st):
                    j, cmin, _, _ = st
                    return jnp.logical_and(j >= 0, cmin <= SB_SKIP_LOG2)

                def body(st):
                    j, _, carry, acc = st
                    lb, sp = soft(scores(qp, j, p), False)
                    w, carry = weights(lb, sp, _dot(sp.astype(BF16), strict), carry)
                    return j - 1, jnp.min(carry), carry, acc + _dot(w, v_ref[rows(j), lanes(p)])

                lb, sp = soft(scores(qp, i, p), True)
                w, carry = weights(lb, sp, _dot(sp.astype(BF16), strict), None)
                st = lax.while_loop(cond, body, (i - 1, jnp.min(carry), carry,
                                                 _dot(w, v_ref[rows(i), lanes(p)])))
                finish(i, p, st[3])
        return 0

    lax.fori_loop(fused, n_blk, fix, 0)


def _sb_attn(q, k, v, bsz, seq, blk=128, pairs=1, fused=3):
    d = q.shape[-1]
    dh = d // SB_HEADS
    gw = pairs * 2 * dh
    assert 2 * dh == LANES and seq % blk == 0 and d % gw == 0
    q3, k3, v3 = (t.reshape(bsz, seq, d) for t in (q, k, v))
    spec = pl.BlockSpec((None, seq, gw), lambda b, p: (b, 0, p))
    return pl.pallas_call(
        functools.partial(_sb_attn_kernel, seq=seq, blk=blk, dh=dh, pairs=pairs, fused=fused),
        grid=(bsz, d // gw),
        in_specs=[spec, spec, spec],
        out_specs=spec,
        out_shape=jax.ShapeDtypeStruct((bsz, seq, d), BF16),
        scratch_shapes=[pltpu.SMEM((seq // blk * pairs,), F32)],
        compiler_params=_params(2),
        name="sb_attn",
    )(q3, k3, v3)


def _sb_out_kernel(o_ref, g_ref, x_ref, cond_ref, fgain_ref, w_ref, out_ref, w_s, *, d):
    @pl.when(pl.program_id(0) == 0)
    def _():
        w_s[...] = w_ref[...].astype(BF16)

    gate = cond_ref[:, 2 * d:3 * d]
    y = (o_ref[...].astype(F32) * _silu(g_ref[...].astype(F32))).astype(BF16)
    x2 = x_ref[...] + gate * _dot(y, w_s[...])
    out_ref[...] = _normed(x2) * fgain_ref[...]


def _sb_out(o2, g, x1, cond, fgain, w_out, seq, tm=1024):
    n, d = x1.shape
    assert seq % tm == 0
    tpb = seq // tm
    row = lambda i: (i, 0)
    fixed = lambda i: (0, 0)
    return pl.pallas_call(
        functools.partial(_sb_out_kernel, d=d),
        grid=(n // tm,),
        in_specs=[pl.BlockSpec((tm, d), row), pl.BlockSpec((tm, d), row), pl.BlockSpec((tm, d), row),
                  pl.BlockSpec((None, 1, 3 * d), lambda i: (i // tpb, 0, 0)),
                  pl.BlockSpec((1, d), fixed),
                  pl.BlockSpec((d, d), fixed, pipeline_mode=pl.Buffered(1))],
        out_specs=pl.BlockSpec((tm, d), row),
        out_shape=jax.ShapeDtypeStruct((n, d), F32),
        scratch_shapes=[pltpu.VMEM((d, d), BF16)],
        compiler_params=_params(1),
        name="sb_out",
    )(o2, g, x1, cond, fgain, w_out)


def kernel(x, c, norm_gain, w_ada, b_ada, gla_w_in, gla_w_gk2, gla_b_gk, gla_o_gain, gla_w_out,
           kv_gain, kv_w_ada, kv_b_ada, w_kv, sb_w_in, sb_w_out, final_gain):
    bsz, seq, d = x.shape
    assert w_ada.shape[0] == 2 and gla_w_in.shape[0] == 1 and sb_w_in.shape[0] == 1
    dk = gla_w_gk2.shape[-1]
    n = bsz * seq
    x2 = x.reshape(n, d)

    cond = _ada(c, w_ada, b_ada)
    cond_kv = _ada(c, kv_w_ada[None], kv_b_ada[None])[0]
    cond1 = cond[0].reshape(bsz, 1, 3 * d)
    cond2 = cond[1].reshape(bsz, 1, 3 * d)
    cond_kv = cond_kv.reshape(bsz, 1, 2 * d)

    w_in = jnp.swapaxes(gla_w_in[0], 0, 1)
    n_main = 2 * dk + 2 * d
    w_lr = jnp.pad(w_in[n_main:], ((0, LANES - GLA_GATE_RANK), (0, 0))).astype(BF16)
    w_gk2 = jnp.pad(gla_w_gk2[0], ((0, LANES - GLA_GATE_RANK), (0, 0))).astype(BF16)
    qe, ke, v, g, dec = _gla_proj(x2, cond1, norm_gain[0:1], w_in, w_lr, w_gk2,
                                  gla_b_gk[0:1], seq)
    x1 = _gla_core(qe, ke, v, dec, g, x2, cond1, gla_o_gain[0:1], gla_w_out[0], bsz, seq)

    dh = d // SB_HEADS
    qscale = (dh ** -0.5) * LOG2E
    ks, vs, qs, g2 = _kvq_proj(x1, cond_kv, cond2, kv_gain.reshape(1, d), norm_gain[1:2],
                               w_kv, sb_w_in[0], seq, qscale)
    o2 = _sb_attn(qs, ks, vs, bsz, seq)
    out = _sb_out(o2.reshape(n, d), g2, x1, cond2, final_gain.reshape(1, d),
                  sb_w_out[0], seq)
    return out.reshape(bsz, seq, d)
```

```python
import functools

import jax
import jax.numpy as jnp
from jax import lax
from jax.experimental import pallas as pl
from jax.experimental.pallas import tpu as pltpu

F32 = jnp.float32
BF16 = jnp.bfloat16

RMS_EPS = 1e-6
GLA_HEADS = 4
GLA_CHUNK = 64
GLA_GATE_RANK = 16
GLA_GATE_NORMALIZER = 16.0
SB_HEADS = 16
LANES = 128
MXU_TILE = 256
LOG2E = 1.4426950408889634
VMEM_LIMIT = 48 * 1024 * 1024
VMEM_LIMIT_LARGE = 56 * 1024 * 1024
SB_SKIP_LOG2 = 160.0
SB_MASKED_Z = -1e30
SB_EXP2_CLAMP = 126.0


def _dot(a, b):
    return jnp.dot(a, b, preferred_element_type=F32)


def _dot_nt(a, b):
    return lax.dot_general(a, b, (((1,), (1,)), ((), ())), preferred_element_type=F32)


def _silu(x):
    return x * jax.nn.sigmoid(x)


def _normed(x):
    return x * lax.rsqrt(jnp.mean(x * x, axis=-1, keepdims=True) + RMS_EPS)


def _split_bf16(x):
    hi = x.astype(BF16)
    return hi, (x - hi.astype(F32)).astype(BF16)


def _params(n_axes, vmem_limit=VMEM_LIMIT):
    return pltpu.CompilerParams(dimension_semantics=("arbitrary",) * n_axes,
                                vmem_limit_bytes=vmem_limit)


def _ada_kernel(c_ref, w_ref, b_ref, o_ref):
    hi, lo = _split_bf16(_silu(c_ref[...]))
    w = w_ref[...].astype(BF16)
    o_ref[...] = _dot(hi, w) + _dot(lo, w) + b_ref[...]


def _ada(c, w, b, tn=1024):
    n_l, d, n = w.shape
    bsz = c.shape[0]
    return pl.pallas_call(
        _ada_kernel,
        grid=(n_l, n // tn),
        in_specs=[pl.BlockSpec((bsz, d), lambda l, j: (0, 0)),
                  pl.BlockSpec((None, d, tn), lambda l, j: (l, 0, j)),
                  pl.BlockSpec((None, 1, tn), lambda l, j: (l, 0, j))],
        out_specs=pl.BlockSpec((None, bsz, tn), lambda l, j: (l, 0, j)),
        out_shape=jax.ShapeDtypeStruct((n_l, bsz, n), F32),
        compiler_params=_params(2),
        name="ada",
    )(c, w, b.reshape(n_l, 1, n))


def _gla_proj_kernel(x_ref, cond_ref, gain_ref, w_ref, wlr_ref, wgk_ref, bgk_ref,
                     qe_ref, ke_ref, v_ref, g_ref, dec_ref, w_s, *, d, dk, tm, qscale):
    n_main = 2 * dk + 2 * d

    @pl.when(pl.program_id(0) == 0)
    def _():
        w_s[...] = w_ref[0:n_main, :].astype(BF16)

    c = GLA_CHUNK
    half = MXU_TILE
    sub = 2 * half
    shift = cond_ref[:, 0:d]
    scale = cond_ref[:, d:2 * d]
    ri = lax.broadcasted_iota(jnp.int32, (half, half), 0)
    ci = lax.broadcasted_iota(jnp.int32, (half, half), 1)
    same_chunk = (ri // c) == (ci // c)
    tril = jnp.where(jnp.logical_and(same_chunk, ci <= ri), 1.0, 0.0).astype(BF16)
    tril2 = jnp.concatenate([tril, tril], axis=1)
    n_c = sub // c
    cj = lax.broadcasted_iota(jnp.int32, (n_c, sub), 1) // c
    cr = lax.broadcasted_iota(jnp.int32, (n_c, sub), 0)
    sel = jnp.where(cj == cr, 1.0, 0.0).astype(BF16)

    for r in range(tm // sub):
        r0 = r * sub
        h = (_normed(x_ref[r0:r0 + sub, :]) * gain_ref[...] * (1.0 + scale) + shift).astype(BF16)
        lr = jnp.concatenate([_dot_nt(h[:half], wlr_ref[...]), _dot_nt(h[half:], wlr_ref[...])],
                             axis=0).astype(BF16)
        pre = _dot(lr, wgk_ref[...]) + bgk_ref[...]
        q = _dot_nt(h, w_s[0:dk, :])
        k = _dot_nt(h, w_s[dk:2 * dk, :])
        gk = jax.nn.log_sigmoid(pre) * (1.0 / GLA_GATE_NORMALIZER)
        hi, lo = _split_bf16(gk)
        b_last = _dot(jnp.concatenate([sel, sel], axis=1), jnp.concatenate([hi, lo], axis=0))
        dec_ref[r * n_c:(r + 1) * n_c, :] = jnp.exp(b_last)
        bs = [_dot(tril2, jnp.concatenate([hi[s * half:(s + 1) * half], lo[s * half:(s + 1) * half]], axis=0))
              for s in range(sub // half)]
        v_ref[r0:r0 + sub, :] = _dot_nt(h, w_s[2 * dk:2 * dk + d, :]).astype(BF16)
        g_ref[r0:r0 + sub, :] = _dot_nt(h, w_s[2 * dk + d:2 * dk + 2 * d, :]).astype(BF16)
        for s, b in enumerate(bs):
            rows = slice(s * half, (s + 1) * half)
            out_rows = slice(r0 + s * half, r0 + (s + 1) * half)
            qe_ref[out_rows, :] = (q[rows] * qscale * jnp.exp(b)).astype(BF16)
            ke_ref[out_rows, :] = (k[rows] * jnp.exp(-b)).astype(BF16)


def _gla_proj(x2, cond, gain, w_in, w_lr, w_gk2, b_gk, seq, tm=1024):
    n, d = x2.shape
    dk = w_gk2.shape[1]
    assert seq % tm == 0 and tm % (2 * MXU_TILE) == 0 and MXU_TILE % GLA_CHUNK == 0
    tpb = seq // tm
    n_c = tm // GLA_CHUNK
    row = lambda i: (i, 0)
    fixed = lambda i: (0, 0)
    qscale = (dk // GLA_HEADS) ** -0.5
    return pl.pallas_call(
        functools.partial(_gla_proj_kernel, d=d, dk=dk, tm=tm, qscale=qscale),
        grid=(n // tm,),
        in_specs=[pl.BlockSpec((tm, d), row),
                  pl.BlockSpec((None, 1, 3 * d), lambda i: (i // tpb, 0, 0)),
                  pl.BlockSpec((1, d), fixed),
                  pl.BlockSpec(w_in.shape, fixed, pipeline_mode=pl.Buffered(1)),
                  pl.BlockSpec(w_lr.shape, fixed),
                  pl.BlockSpec(w_gk2.shape, fixed),
                  pl.BlockSpec((1, dk), fixed)],
        out_specs=[pl.BlockSpec((tm, dk), row), pl.BlockSpec((tm, dk), row),
                   pl.BlockSpec((tm, d), row), pl.BlockSpec((tm, d), row),
                   pl.BlockSpec((n_c, dk), row)],
        out_shape=[jax.ShapeDtypeStruct((n, dk), BF16), jax.ShapeDtypeStruct((n, dk), BF16),
                   jax.ShapeDtypeStruct((n, d), BF16), jax.ShapeDtypeStruct((n, d), BF16),
                   jax.ShapeDtypeStruct((n // GLA_CHUNK, dk), F32)],
        scratch_shapes=[pltpu.VMEM((2 * dk + 2 * d, d), BF16)],
        compiler_params=_params(1),
        name="gla_proj",
    )(x2, cond, gain, w_in, w_lr, w_gk2, b_gk)


def _gla_core_kernel(qe_ref, ke_ref, v_ref, dec_ref, g_ref, x_ref, cond_ref, ogain_ref, w_ref,
                     x1_ref, s_ref, w_s, y_s, *, tile, tiles_per_seq, d, hk, hv):
    c = GLA_CHUNK
    n_c = tile // c
    heads = range(GLA_HEADS)
    ri = lax.broadcasted_iota(jnp.int32, (c, c), 0)
    ci = lax.broadcasted_iota(jnp.int32, (c, c), 1)
    causal = ci <= ri

    @pl.when(pl.program_id(0) == 0)
    def _():
        w_s[...] = w_ref[...].astype(BF16)

    @pl.when(pl.program_id(0) % tiles_per_seq == 0)
    def _():
        s_ref[...] = jnp.zeros(s_ref.shape, F32)

    def chunk(ref, h, u, width):
        return ref[u * c:(u + 1) * c, h * width:(h + 1) * width]

    work = [(h, u) for u in range(n_c) for h in heads]
    qs = {hu: chunk(qe_ref, *hu, hk) for hu in work}
    ks = {hu: chunk(ke_ref, *hu, hk) for hu in work}
    vs = {hu: chunk(v_ref, *hu, hv) for hu in work}
    scores = {hu: _dot_nt(qs[hu], ks[hu]) for hu in work}
    masked = {hu: jnp.where(causal, scores[hu], 0.0).astype(BF16) for hu in work}
    dec_t = [dec_ref[:, h * hk:(h + 1) * hk].T for h in heads]
    s = [s_ref[h] for h in heads]
    for u in range(n_c):
        rows = slice(u * c, (u + 1) * c)
        incs = [_dot(ks[h, u].T, vs[h, u]) for h in heads]
        o = []
        for h in heads:
            lhs = jnp.concatenate([qs[h, u], masked[h, u]], axis=1)
            rhs = jnp.concatenate([s[h].astype(BF16), vs[h, u]], axis=0)
            o.append(_dot(lhs, rhs))
            s[h] = (s[h] + incs[h]) * dec_t[h][:, u:u + 1]
        y = jnp.concatenate([_normed(o[h]) * ogain_ref[...] for h in heads], axis=1)
        y_s[rows, :] = (y * _silu(g_ref[rows, :].astype(F32))).astype(BF16)
    for h in heads:
        s_ref[h] = s[h]

    x1_ref[...] = x_ref[...] + cond_ref[:, 2 * d:3 * d] * _dot(y_s[...], w_s[...])


def _gla_core(qe, ke, v, dec, g, x2, cond, ogain, w_out, bsz, seq, tile=1024):
    d = x2.shape[-1]
    dk = qe.shape[-1]
    assert seq % tile == 0 and tile % GLA_CHUNK == 0 and (tile // GLA_CHUNK) % 8 == 0
    n_c = tile // GLA_CHUNK
    tpb = seq // tile
    row = lambda i: (i, 0)
    fixed = lambda i: (0, 0)
    return pl.pallas_call(
        functools.partial(_gla_core_kernel, tile=tile, tiles_per_seq=tpb, d=d, hk=dk // GLA_HEADS,
                          hv=d // GLA_HEADS),
        grid=(bsz * tpb,),
        in_specs=[pl.BlockSpec((tile, dk), row), pl.BlockSpec((tile, dk), row),
                  pl.BlockSpec((tile, d), row), pl.BlockSpec((n_c, dk), row),
                  pl.BlockSpec((tile, d), row), pl.BlockSpec((tile, d), row),
                  pl.BlockSpec((None, 1, 3 * d), lambda i: (i // tpb, 0, 0)),
                  pl.BlockSpec((1, d // GLA_HEADS), fixed),
                  pl.BlockSpec((d, d), fixed, pipeline_mode=pl.Buffered(1))],
        out_specs=pl.BlockSpec((tile, d), row),
        out_shape=jax.ShapeDtypeStruct((bsz * seq, d), F32),
        scratch_shapes=[pltpu.VMEM((GLA_HEADS, dk // GLA_HEADS, d // GLA_HEADS), F32),
                        pltpu.VMEM((d, d), BF16), pltpu.VMEM((tile, d), BF16)],
        compiler_params=_params(1),
        name="gla_core",
    )(qe, ke, v, dec, g, x2, cond, ogain, w_out)


def _kvq_proj_kernel(x_ref, ckv_ref, c2_ref, kvgain_ref, gain_ref, wkv_ref, win_ref,
                     k_ref, v_ref, q_ref, g_ref, wkv_s, win_s, *, d, qscale, sub):
    @pl.when(pl.program_id(0) == 0)
    def _():
        wkv_s[...] = wkv_ref[...].astype(BF16)
        win_s[...] = win_ref[...].astype(BF16)

    for r in range(x_ref.shape[0] // sub):
        rows = slice(r * sub, (r + 1) * sub)
        y = _normed(x_ref[rows, :])
        hkv = (y * kvgain_ref[...] * (1.0 + ckv_ref[:, d:2 * d]) + ckv_ref[:, 0:d]).astype(BF16)
        h2 = (y * gain_ref[...] * (1.0 + c2_ref[:, d:2 * d]) + c2_ref[:, 0:d]).astype(BF16)
        k_ref[rows, :] = _dot(hkv, wkv_s[:, 0:d]).astype(BF16)
        v_ref[rows, :] = _dot(hkv, wkv_s[:, d:2 * d]).astype(BF16)
        q_ref[rows, :] = (_dot(h2, win_s[:, 0:d]) * qscale).astype(BF16)
        g_ref[rows, :] = _dot(h2, win_s[:, d:2 * d]).astype(BF16)


def _kvq_proj(x1, ckv, c2, kvgain, gain, w_kv, w_in, seq, qscale, tm=1024, sub=512):
    n, d = x1.shape
    assert seq % tm == 0 and tm % sub == 0
    tpb = seq // tm
    row = lambda i: (i, 0)
    fixed = lambda i: (0, 0)
    once = dict(pipeline_mode=pl.Buffered(1))
    return pl.pallas_call(
        functools.partial(_kvq_proj_kernel, d=d, qscale=qscale, sub=sub),
        grid=(n // tm,),
        in_specs=[pl.BlockSpec((tm, d), row),
                  pl.BlockSpec((None, 1, 2 * d), lambda i: (i // tpb, 0, 0)),
                  pl.BlockSpec((None, 1, 3 * d), lambda i: (i // tpb, 0, 0)),
                  pl.BlockSpec((1, d), fixed), pl.BlockSpec((1, d), fixed),
                  pl.BlockSpec((d, 2 * d), fixed, **once), pl.BlockSpec((d, 2 * d), fixed, **once)],
        out_specs=[pl.BlockSpec((tm, d), row)] * 4,
        out_shape=[jax.ShapeDtypeStruct((n, d), BF16)] * 4,
        scratch_shapes=[pltpu.VMEM((d, 2 * d), BF16), pltpu.VMEM((d, 2 * d), BF16)],
        compiler_params=_params(1, VMEM_LIMIT_LARGE),
        name="kvq_proj",
    )(x1, ckv, c2, kvgain, gain, w_kv, w_in)


def _sb_attn_kernel(q_ref, k_ref, v_ref, o_ref, cmin_s, *, seq, blk, dh, pairs, fused):
    n_blk = seq // blk
    pw = 2 * dh
    lane = lax.broadcasted_iota(jnp.int32, (blk, pw), 1)
    ri = lax.broadcasted_iota(jnp.int32, (blk, blk), 0)
    ci = lax.broadcasted_iota(jnp.int32, (blk, blk), 1)
    strict = jnp.where(ci < ri, 1.0, 0.0).astype(BF16)
    ri2 = lax.broadcasted_iota(jnp.int32, (2 * blk, blk), 0)
    ci2 = lax.broadcasted_iota(jnp.int32, (2 * blk, blk), 1)
    below = ci2 < (ri2 & (blk - 1))

    def rows(j):
        return pl.ds(pl.multiple_of(j * blk, blk), blk)

    def lanes(p):
        return slice(p * pw, (p + 1) * pw)

    def stacked_q(i, p):
        q2 = q_ref[rows(i), lanes(p)]
        zero = jnp.zeros_like(q2)
        return jnp.concatenate([jnp.where(lane < dh, q2, zero), jnp.where(lane >= dh, q2, zero)], axis=0)

    def scores(qp, j, p):
        return _dot_nt(qp, k_ref[rows(j), lanes(p)])

    def soft(z, diagonal):
        if diagonal:
            z = jnp.where(below, z, SB_MASKED_Z)
        sp = jnp.maximum(jnp.log2(1.0 + jnp.exp2(jnp.minimum(z, SB_EXP2_CLAMP))), z)
        return z - sp, sp

    def weights(lb, sp, sums, carry):
        tot = sums if carry is None else sums + jnp.concatenate([carry] * (blk // LANES), axis=1)
        w = jnp.exp2(lb - tot).astype(BF16)
        return w, jnp.broadcast_to(tot[:, 0:1] + sp[:, 0:1], (2 * blk, LANES))

    def finish(i, p, acc):
        o_ref[rows(i), lanes(p)] = jnp.where(lane < dh, acc[:blk], acc[blk:]).astype(o_ref.dtype)

    def qk_stage(i):
        qs = [stacked_q(i, p) for p in range(pairs)]
        return {(p, n): scores(qs[p], i - n, p) for n in range(min(i + 1, fused)) for p in range(pairs)}

    zs = qk_stage(0)
    lowest = None
    for i in range(n_blk):
        work = sorted(zs, key=lambda pn: (pn[1], pn[0]))
        soft_i = {key: soft(zs[key], key[1] == 0) for key in work}
        sums = {key: _dot(soft_i[key][1].astype(BF16), strict) for key in work}
        if i + 1 < n_blk:
            zs = qk_stage(i + 1)
        carries = [None] * pairs
        ws = {}
        for p, n in work:
            ws[p, n], carries[p] = weights(*soft_i[p, n], sums[p, n], carries[p])
        for p in range(pairs):
            acc = _dot(ws[p, 0], v_ref[rows(i), lanes(p)])
            for n in range(1, min(i + 1, fused)):
                acc = acc + _dot(ws[p, n], v_ref[rows(i - n), lanes(p)])
            finish(i, p, acc)
            cmin = jnp.min(carries[p])
            cmin_s[i * pairs + p] = cmin
            if i >= fused:
                lowest = cmin if lowest is None else jnp.minimum(lowest, cmin)

    def fix(i, _):
        for p in range(pairs):
            @pl.when(cmin_s[i * pairs + p] <= SB_SKIP_LOG2)
            def _():
                qp = stacked_q(i, p)

                def cond(st):
                    j, cmin, _, _ = st
                    return jnp.logical_and(j >= 0, cmin <= SB_SKIP_LOG2)

                def body(st):
                    j, _, carry, acc = st
                    lb, sp = soft(scores(qp, j, p), False)
                    w, carry = weights(lb, sp, _dot(sp.astype(BF16), strict), carry)
                    return j - 1, jnp.min(carry), carry, acc + _dot(w, v_ref[rows(j), lanes(p)])

                lb, sp = soft(scores(qp, i, p), True)
                w, carry = weights(lb, sp, _dot(sp.astype(BF16), strict), None)
                st = lax.while_loop(cond, body, (i - 1, jnp.min(carry), carry,
                                                 _dot(w, v_ref[rows(i), lanes(p)])))
                finish(i, p, st[3])
        return 0

    if lowest is not None:
        @pl.when(lowest <= SB_SKIP_LOG2)
        def _():
            lax.fori_loop(fused, n_blk, fix, 0)


def _sb_attn(q, k, v, bsz, seq, blk=128, pairs=1, fused=3):
    d = q.shape[-1]
    dh = d // SB_HEADS
    gw = pairs * 2 * dh
    assert 2 * dh == LANES and seq % blk == 0 and d % gw == 0
    q3, k3, v3 = (t.reshape(bsz, seq, d) for t in (q, k, v))
    spec = pl.BlockSpec((None, seq, gw), lambda b, p: (b, 0, p))
    return pl.pallas_call(
        functools.partial(_sb_attn_kernel, seq=seq, blk=blk, dh=dh, pairs=pairs, fused=fused),
        grid=(bsz, d // gw),
        in_specs=[spec, spec, spec],
        out_specs=spec,
        out_shape=jax.ShapeDtypeStruct((bsz, seq, d), BF16),
        scratch_shapes=[pltpu.SMEM((seq // blk * pairs,), F32)],
        compiler_params=_params(2),
        name="sb_attn",
    )(q3, k3, v3)


def _sb_out_kernel(o_ref, g_ref, x_ref, cond_ref, fgain_ref, w_ref, out_ref, w_s, *, d):
    @pl.when(pl.program_id(0) == 0)
    def _():
        w_s[...] = w_ref[...].astype(BF16)

    gate = cond_ref[:, 2 * d:3 * d]
    y = (o_ref[...].astype(F32) * _silu(g_ref[...].astype(F32))).astype(BF16)
    x2 = x_ref[...] + gate * _dot(y, w_s[...])
    out_ref[...] = _normed(x2) * fgain_ref[...]


def _sb_out(o2, g, x1, cond, fgain, w_out, seq, tm=1024):
    n, d = x1.shape
    assert seq % tm == 0
    tpb = seq // tm
    row = lambda i: (i, 0)
    fixed = lambda i: (0, 0)
    return pl.pallas_call(
        functools.partial(_sb_out_kernel, d=d),
        grid=(n // tm,),
        in_specs=[pl.BlockSpec((tm, d), row), pl.BlockSpec((tm, d), row), pl.BlockSpec((tm, d), row),
                  pl.BlockSpec((None, 1, 3 * d), lambda i: (i // tpb, 0, 0)),
                  pl.BlockSpec((1, d), fixed),
                  pl.BlockSpec((d, d), fixed, pipeline_mode=pl.Buffered(1))],
        out_specs=pl.BlockSpec((tm, d), row),
        out_shape=jax.ShapeDtypeStruct((n, d), F32),
        scratch_shapes=[pltpu.VMEM((d, d), BF16)],
        compiler_params=_params(1),
        name="sb_out",
    )(o2, g, x1, cond, fgain, w_out)


def kernel(x, c, norm_gain, w_ada, b_ada, gla_w_in, gla_w_gk2, gla_b_gk, gla_o_gain, gla_w_out,
           kv_gain, kv_w_ada, kv_b_ada, w_kv, sb_w_in, sb_w_out, final_gain):
    bsz, seq, d = x.shape
    assert w_ada.shape[0] == 2 and gla_w_in.shape[0] == 1 and sb_w_in.shape[0] == 1
    dk = gla_w_gk2.shape[-1]
    n = bsz * seq
    x2 = x.reshape(n, d)

    cond = _ada(c, w_ada, b_ada)
    cond_kv = _ada(c, kv_w_ada[None], kv_b_ada[None])[0]
    cond1 = cond[0].reshape(bsz, 1, 3 * d)
    cond2 = cond[1].reshape(bsz, 1, 3 * d)
    cond_kv = cond_kv.reshape(bsz, 1, 2 * d)

    w_in = jnp.swapaxes(gla_w_in[0], 0, 1)
    n_main = 2 * dk + 2 * d
    w_lr = jnp.pad(w_in[n_main:], ((0, LANES - GLA_GATE_RANK), (0, 0))).astype(BF16)
    w_gk2 = jnp.pad(gla_w_gk2[0], ((0, LANES - GLA_GATE_RANK), (0, 0))).astype(BF16)
    qe, ke, v, g, dec = _gla_proj(x2, cond1, norm_gain[0:1], w_in, w_lr, w_gk2,
                                  gla_b_gk[0:1], seq)
    x1 = _gla_core(qe, ke, v, dec, g, x2, cond1, gla_o_gain[0:1], gla_w_out[0], bsz, seq)

    dh = d // SB_HEADS
    qscale = (dh ** -0.5) * LOG2E
    ks, vs, qs, g2 = _kvq_proj(x1, cond_kv, cond2, kv_gain.reshape(1, d), norm_gain[1:2],
                               w_kv, sb_w_in[0], seq, qscale)
    o2 = _sb_attn(qs, ks, vs, bsz, seq)
    out = _sb_out(o2.reshape(n, d), g2, x1, cond2, final_gain.reshape(1, d),
                  sb_w_out[0], seq)
    return out.reshape(bsz, seq, d)
```

```python
import functools

import jax
import jax.numpy as jnp
from jax import lax
from jax.experimental import pallas as pl
from jax.experimental.pallas import tpu as pltpu

F32 = jnp.float32
BF16 = jnp.bfloat16

RMS_EPS = 1e-6
GLA_HEADS = 4
GLA_CHUNK = 64
GLA_GATE_RANK = 16
GLA_GATE_NORMALIZER = 16.0
SB_HEADS = 16
LANES = 128
MXU_TILE = 256
LOG2E = 1.4426950408889634
VMEM_LIMIT = 48 * 1024 * 1024
VMEM_LIMIT_LARGE = 56 * 1024 * 1024
SB_SKIP_LOG2 = 160.0
SB_MASKED_Z = -1e30
SB_EXP2_CLAMP = 126.0


def _dot(a, b):
    return jnp.dot(a, b, preferred_element_type=F32)


def _dot_nt(a, b):
    return lax.dot_general(a, b, (((1,), (1,)), ((), ())), preferred_element_type=F32)


def _silu(x):
    return x * jax.nn.sigmoid(x)


def _normed(x):
    return x * lax.rsqrt(jnp.mean(x * x, axis=-1, keepdims=True) + RMS_EPS)


def _split_bf16(x):
    hi = x.astype(BF16)
    return hi, (x - hi.astype(F32)).astype(BF16)


def _params(n_axes, vmem_limit=VMEM_LIMIT):
    return pltpu.CompilerParams(dimension_semantics=("arbitrary",) * n_axes,
                                vmem_limit_bytes=vmem_limit)


def _ada_kernel(c_ref, w_ref, b_ref, o_ref):
    hi, lo = _split_bf16(_silu(c_ref[...]))
    w = w_ref[...].astype(BF16)
    o_ref[...] = _dot(hi, w) + _dot(lo, w) + b_ref[...]


def _ada(c, w, b, tn=1024):
    n_l, d, n = w.shape
    bsz = c.shape[0]
    return pl.pallas_call(
        _ada_kernel,
        grid=(n_l, n // tn),
        in_specs=[pl.BlockSpec((bsz, d), lambda l, j: (0, 0)),
                  pl.BlockSpec((None, d, tn), lambda l, j: (l, 0, j)),
                  pl.BlockSpec((None, 1, tn), lambda l, j: (l, 0, j))],
        out_specs=pl.BlockSpec((None, bsz, tn), lambda l, j: (l, 0, j)),
        out_shape=jax.ShapeDtypeStruct((n_l, bsz, n), F32),
        compiler_params=_params(2),
        name="ada",
    )(c, w, b.reshape(n_l, 1, n))


def _gla_proj_kernel(x_ref, cond_ref, gain_ref, w_ref, wlr_ref, wgk_ref, bgk_ref,
                     qe_ref, ke_ref, v_ref, g_ref, dec_ref, w_s, *, d, dk, tm, qscale):
    n_main = 2 * dk + 2 * d

    @pl.when(pl.program_id(0) == 0)
    def _():
        w_s[...] = w_ref[0:n_main, :].astype(BF16)

    c = GLA_CHUNK
    half = MXU_TILE
    sub = 2 * half
    shift = cond_ref[:, 0:d]
    scale = cond_ref[:, d:2 * d]
    ri = lax.broadcasted_iota(jnp.int32, (half, half), 0)
    ci = lax.broadcasted_iota(jnp.int32, (half, half), 1)
    same_chunk = (ri // c) == (ci // c)
    tril = jnp.where(jnp.logical_and(same_chunk, ci <= ri), 1.0, 0.0).astype(BF16)
    tril2 = jnp.concatenate([tril, tril], axis=1)
    n_c = sub // c
    cj = lax.broadcasted_iota(jnp.int32, (n_c, sub), 1) // c
    cr = lax.broadcasted_iota(jnp.int32, (n_c, sub), 0)
    sel = jnp.where(cj == cr, 1.0, 0.0).astype(BF16)

    for r in range(tm // sub):
        r0 = r * sub
        h = (_normed(x_ref[r0:r0 + sub, :]) * gain_ref[...] * (1.0 + scale) + shift).astype(BF16)
        lr = jnp.concatenate([_dot_nt(h[:half], wlr_ref[...]), _dot_nt(h[half:], wlr_ref[...])],
                             axis=0).astype(BF16)
        pre = _dot(lr, wgk_ref[...]) + bgk_ref[...]
        q = _dot_nt(h, w_s[0:dk, :])
        k = _dot_nt(h, w_s[dk:2 * dk, :])
        gk = jax.nn.log_sigmoid(pre) * (1.0 / GLA_GATE_NORMALIZER)
        hi, lo = _split_bf16(gk)
        b_last = _dot(jnp.concatenate([sel, sel], axis=1), jnp.concatenate([hi, lo], axis=0))
        dec_ref[r * n_c:(r + 1) * n_c, :] = jnp.exp(b_last)
        bs = [_dot(tril2, jnp.concatenate([hi[s * half:(s + 1) * half], lo[s * half:(s + 1) * half]], axis=0))
              for s in range(sub // half)]
        v_ref[r0:r0 + sub, :] = _dot_nt(h, w_s[2 * dk:2 * dk + d, :]).astype(BF16)
        g_ref[r0:r0 + sub, :] = _dot_nt(h, w_s[2 * dk + d:2 * dk + 2 * d, :]).astype(BF16)
        for s, b in enumerate(bs):
            rows = slice(s * half, (s + 1) * half)
            out_rows = slice(r0 + s * half, r0 + (s + 1) * half)
            qe_ref[out_rows, :] = (q[rows] * qscale * jnp.exp(b)).astype(BF16)
            ke_ref[out_rows, :] = (k[rows] * jnp.exp(-b)).astype(BF16)


def _gla_proj(x2, cond, gain, w_in, w_lr, w_gk2, b_gk, seq, tm=1024):
    n, d = x2.shape
    dk = w_gk2.shape[1]
    assert seq % tm == 0 and tm % (2 * MXU_TILE) == 0 and MXU_TILE % GLA_CHUNK == 0
    tpb = seq // tm
    n_c = tm // GLA_CHUNK
    row = lambda i: (i, 0)
    fixed = lambda i: (0, 0)
    qscale = (dk // GLA_HEADS) ** -0.5
    return pl.pallas_call(
        functools.partial(_gla_proj_kernel, d=d, dk=dk, tm=tm, qscale=qscale),
        grid=(n // tm,),
        in_specs=[pl.BlockSpec((tm, d), row),
                  pl.BlockSpec((None, 1, 3 * d), lambda i: (i // tpb, 0, 0)),
                  pl.BlockSpec((1, d), fixed),
                  pl.BlockSpec(w_in.shape, fixed, pipeline_mode=pl.Buffered(1)),
                  pl.BlockSpec(w_lr.shape, fixed),
                  pl.BlockSpec(w_gk2.shape, fixed),
                  pl.BlockSpec((1, dk), fixed)],
        out_specs=[pl.BlockSpec((tm, dk), row), pl.BlockSpec((tm, dk), row),
                   pl.BlockSpec((tm, d), row), pl.BlockSpec((tm, d), row),
                   pl.BlockSpec((n_c, dk), row)],
        out_shape=[jax.ShapeDtypeStruct((n, dk), BF16), jax.ShapeDtypeStruct((n, dk), BF16),
                   jax.ShapeDtypeStruct((n, d), BF16), jax.ShapeDtypeStruct((n, d), BF16),
                   jax.ShapeDtypeStruct((n // GLA_CHUNK, dk), F32)],
        scratch_shapes=[pltpu.VMEM((2 * dk + 2 * d, d), BF16)],
        compiler_params=_params(1),
        name="gla_proj",
    )(x2, cond, gain, w_in, w_lr, w_gk2, b_gk)


def _gla_core_kernel(qe_ref, ke_ref, v_ref, dec_ref, g_ref, x_ref, cond_ref, ogain_ref, w_ref,
                     x1_ref, s_ref, w_s, y_s, *, tile, tiles_per_seq, d, hk, hv):
    c = GLA_CHUNK
    n_c = tile // c
    heads = range(GLA_HEADS)
    ri = lax.broadcasted_iota(jnp.int32, (c, c), 0)
    ci = lax.broadcasted_iota(jnp.int32, (c, c), 1)
    causal = ci <= ri

    @pl.when(pl.program_id(0) == 0)
    def _():
        w_s[...] = w_ref[...].astype(BF16)

    @pl.when(pl.program_id(0) % tiles_per_seq == 0)
    def _():
        s_ref[...] = jnp.zeros(s_ref.shape, F32)

    def chunk(ref, h, u, width):
        return ref[u * c:(u + 1) * c, h * width:(h + 1) * width]

    work = [(h, u) for u in range(n_c) for h in heads]
    qs = {hu: chunk(qe_ref, *hu, hk) for hu in work}
    ks = {hu: chunk(ke_ref, *hu, hk) for hu in work}
    vs = {hu: chunk(v_ref, *hu, hv) for hu in work}
    scores = {hu: _dot_nt(qs[hu], ks[hu]) for hu in work}
    masked = {hu: jnp.where(causal, scores[hu], 0.0).astype(BF16) for hu in work}
    dec_t = [dec_ref[:, h * hk:(h + 1) * hk].T for h in heads]
    s = [s_ref[h] for h in heads]
    for u in range(n_c):
        rows = slice(u * c, (u + 1) * c)
        incs = [_dot(ks[h, u].T, vs[h, u]) for h in heads]
        o = []
        for h in heads:
            lhs = jnp.concatenate([qs[h, u], masked[h, u]], axis=1)
            rhs = jnp.concatenate([s[h].astype(BF16), vs[h, u]], axis=0)
            o.append(_dot(lhs, rhs))
            s[h] = (s[h] + incs[h]) * dec_t[h][:, u:u + 1]
        y = jnp.concatenate([_normed(o[h]) * ogain_ref[...] for h in heads], axis=1)
        y_s[rows, :] = (y * _silu(g_ref[rows, :].astype(F32))).astype(BF16)
    for h in heads:
        s_ref[h] = s[h]

    x1_ref[...] = x_ref[...] + cond_ref[:, 2 * d:3 * d] * _dot(y_s[...], w_s[...])


def _gla_core(qe, ke, v, dec, g, x2, cond, ogain, w_out, bsz, seq, tile=1024):
    d = x2.shape[-1]
    dk = qe.shape[-1]
    assert seq % tile == 0 and tile % GLA_CHUNK == 0 and (tile // GLA_CHUNK) % 8 == 0
    n_c = tile // GLA_CHUNK
    tpb = seq // tile
    row = lambda i: (i, 0)
    fixed = lambda i: (0, 0)
    return pl.pallas_call(
        functools.partial(_gla_core_kernel, tile=tile, tiles_per_seq=tpb, d=d, hk=dk // GLA_HEADS,
                          hv=d // GLA_HEADS),
        grid=(bsz * tpb,),
        in_specs=[pl.BlockSpec((tile, dk), row), pl.BlockSpec((tile, dk), row),
                  pl.BlockSpec((tile, d), row), pl.BlockSpec((n_c, dk), row),
                  pl.BlockSpec((tile, d), row), pl.BlockSpec((tile, d), row),
                  pl.BlockSpec((None, 1, 3 * d), lambda i: (i // tpb, 0, 0)),
                  pl.BlockSpec((1, d // GLA_HEADS), fixed),
                  pl.BlockSpec((d, d), fixed, pipeline_mode=pl.Buffered(1))],
        out_specs=pl.BlockSpec((tile, d), row),
        out_shape=jax.ShapeDtypeStruct((bsz * seq, d), F32),
        scratch_shapes=[pltpu.VMEM((GLA_HEADS, dk // GLA_HEADS, d // GLA_HEADS), F32),
                        pltpu.VMEM((d, d), BF16), pltpu.VMEM((tile, d), BF16)],
        compiler_params=_params(1),
        name="gla_core",
    )(qe, ke, v, dec, g, x2, cond, ogain, w_out)


def _kvq_proj_kernel(x_ref, ckv_ref, c2_ref, kvgain_ref, gain_ref, wkv_ref, win_ref,
                     k_ref, v_ref, q_ref, g_ref, wkv_s, win_s, *, d, qscale, sub):
    @pl.when(pl.program_id(0) == 0)
    def _():
        wkv_s[...] = wkv_ref[...].astype(BF16)
        win_s[...] = win_ref[...].astype(BF16)

    for r in range(x_ref.shape[0] // sub):
        rows = slice(r * sub, (r + 1) * sub)
        y = _normed(x_ref[rows, :])
        hkv = (y * kvgain_ref[...] * (1.0 + ckv_ref[:, d:2 * d]) + ckv_ref[:, 0:d]).astype(BF16)
        h2 = (y * gain_ref[...] * (1.0 + c2_ref[:, d:2 * d]) + c2_ref[:, 0:d]).astype(BF16)
        k_ref[rows, :] = _dot(hkv, wkv_s[:, 0:d]).astype(BF16)
        v_ref[rows, :] = _dot(hkv, wkv_s[:, d:2 * d]).astype(BF16)
        q_ref[rows, :] = (_dot(h2, win_s[:, 0:d]) * qscale).astype(BF16)
        g_ref[rows, :] = _dot(h2, win_s[:, d:2 * d]).astype(BF16)


def _kvq_proj(x1, ckv, c2, kvgain, gain, w_kv, w_in, seq, qscale, tm=1024, sub=512):
    n, d = x1.shape
    assert seq % tm == 0 and tm % sub == 0
    tpb = seq // tm
    row = lambda i: (i, 0)
    fixed = lambda i: (0, 0)
    once = dict(pipeline_mode=pl.Buffered(1))
    return pl.pallas_call(
        functools.partial(_kvq_proj_kernel, d=d, qscale=qscale, sub=sub),
        grid=(n // tm,),
        in_specs=[pl.BlockSpec((tm, d), row),
                  pl.BlockSpec((None, 1, 2 * d), lambda i: (i // tpb, 0, 0)),
                  pl.BlockSpec((None, 1, 3 * d), lambda i: (i // tpb, 0, 0)),
                  pl.BlockSpec((1, d), fixed), pl.BlockSpec((1, d), fixed),
                  pl.BlockSpec((d, 2 * d), fixed, **once), pl.BlockSpec((d, 2 * d), fixed, **once)],
        out_specs=[pl.BlockSpec((tm, d), row)] * 4,
        out_shape=[jax.ShapeDtypeStruct((n, d), BF16)] * 4,
        scratch_shapes=[pltpu.VMEM((d, 2 * d), BF16), pltpu.VMEM((d, 2 * d), BF16)],
        compiler_params=_params(1, VMEM_LIMIT_LARGE),
        name="kvq_proj",
    )(x1, ckv, c2, kvgain, gain, w_kv, w_in)


def _sb_attn_kernel(q_ref, k_ref, v_ref, o_ref, cmin_s, *, seq, blk, dh, pairs, fused):
    n_blk = seq // blk
    pw = 2 * dh
    lane = lax.broadcasted_iota(jnp.int32, (blk, pw), 1)
    ri = lax.broadcasted_iota(jnp.int32, (blk, blk), 0)
    ci = lax.broadcasted_iota(jnp.int32, (blk, blk), 1)
    strict = jnp.where(ci < ri, 1.0, 0.0).astype(BF16)
    ri2 = lax.broadcasted_iota(jnp.int32, (2 * blk, blk), 0)
    ci2 = lax.broadcasted_iota(jnp.int32, (2 * blk, blk), 1)
    below = ci2 < (ri2 & (blk - 1))

    def rows(j):
        return pl.ds(pl.multiple_of(j * blk, blk), blk)

    def lanes(p):
        return slice(p * pw, (p + 1) * pw)

    def stacked_q(i, p):
        q2 = q_ref[rows(i), lanes(p)]
        zero = jnp.zeros_like(q2)
        return jnp.concatenate([jnp.where(lane < dh, q2, zero), jnp.where(lane >= dh, q2, zero)], axis=0)

    def scores(qp, j, p):
        return _dot_nt(qp, k_ref[rows(j), lanes(p)])

    def soft(z, diagonal):
        if diagonal:
            z = jnp.where(below, z, SB_MASKED_Z)
        sp = jnp.maximum(jnp.log2(1.0 + jnp.exp2(jnp.minimum(z, SB_EXP2_CLAMP))), z)
        return z - sp, sp

    def weights(lb, sp, sums, carry):
        tot = sums if carry is None else sums + jnp.concatenate([carry] * (blk // LANES), axis=1)
        w = jnp.exp2(lb - tot).astype(BF16)
        return w, jnp.broadcast_to(tot[:, 0:1] + sp[:, 0:1], (2 * blk, LANES))

    def finish(i, p, acc):
        o_ref[rows(i), lanes(p)] = jnp.where(lane < dh, acc[:blk], acc[blk:]).astype(o_ref.dtype)

    def qk_stage(i):
        qs = [stacked_q(i, p) for p in range(pairs)]
        return {(p, n): scores(qs[p], i - n, p) for n in range(min(i + 1, fused)) for p in range(pairs)}

    zs = qk_stage(0)
    lowest = None
    for i in range(n_blk):
        work = sorted(zs, key=lambda pn: (pn[1], pn[0]))
        soft_i = {key: soft(zs[key], key[1] == 0) for key in work}
        sums = {key: _dot(soft_i[key][1].astype(BF16), strict) for key in work}
        if i + 1 < n_blk:
            zs = qk_stage(i + 1)
        carries = [None] * pairs
        ws = {}
        for p, n in work:
            ws[p, n], carries[p] = weights(*soft_i[p, n], sums[p, n], carries[p])
        for p in range(pairs):
            acc = _dot(ws[p, 0], v_ref[rows(i), lanes(p)])
            for n in range(1, min(i + 1, fused)):
                acc = acc + _dot(ws[p, n], v_ref[rows(i - n), lanes(p)])
            finish(i, p, acc)
            cmin = jnp.min(carries[p])
            cmin_s[i * pairs + p] = cmin
            if i >= fused:
                lowest = cmin if lowest is None else jnp.minimum(lowest, cmin)

    def fix(i, _):
        for p in range(pairs):
            @pl.when(cmin_s[i * pairs + p] <= SB_SKIP_LOG2)
            def _():
                qp = stacked_q(i, p)

                def cond(st):
                    j, cmin, _, _ = st
                    return jnp.logical_and(j >= 0, cmin <= SB_SKIP_LOG2)

                def body(st):
                    j, _, carry, acc = st
                    lb, sp = soft(scores(qp, j, p), False)
                    w, carry = weights(lb, sp, _dot(sp.astype(BF16), strict), carry)
                    return j - 1, jnp.min(carry), carry, acc + _dot(w, v_ref[rows(j), lanes(p)])

                lb, sp = soft(scores(qp, i, p), True)
                w, carry = weights(lb, sp, _dot(sp.astype(BF16), strict), None)
                st = lax.while_loop(cond, body, (i - 1, jnp.min(carry), carry,
                                                 _dot(w, v_ref[rows(i), lanes(p)])))
                finish(i, p, st[3])
        return 0

    if lowest is not None:
        @pl.when(lowest <= SB_SKIP_LOG2)
        def _():
            lax.fori_loop(fused, n_blk, fix, 0)


def _sb_attn(q, k, v, bsz, seq, blk=128, pairs=1, fused=3):
    d = q.shape[-1]
    dh = d // SB_HEADS
    gw = pairs * 2 * dh
    assert 2 * dh == LANES and seq % blk == 0 and d % gw == 0
    q3, k3, v3 = (t.reshape(bsz, seq, d) for t in (q, k, v))
    spec = pl.BlockSpec((None, seq, gw), lambda b, p: (b, 0, p))
    return pl.pallas_call(
        functools.partial(_sb_attn_kernel, seq=seq, blk=blk, dh=dh, pairs=pairs, fused=fused),
        grid=(bsz, d // gw),
        in_specs=[spec, spec, spec],
        out_specs=spec,
        out_shape=jax.ShapeDtypeStruct((bsz, seq, d), BF16),
        scratch_shapes=[pltpu.SMEM((seq // blk * pairs,), F32)],
        compiler_params=_params(2),
        name="sb_attn",
    )(q3, k3, v3)


def _sb_out_kernel(o_hbm, g_hbm, x_hbm, cond_ref, fgain_ref, w_ref, out_ref,
                   o_buf, g_buf, x_buf, sems, w_s, *, d, tm, n_steps, n_buf):
    s = pl.program_id(0)

    def copies(tile, slot):
        rows = pl.ds(pl.multiple_of(tile * tm, tm), tm)
        return [pltpu.make_async_copy(src.at[rows, :], buf.at[slot], sems.at[k, slot])
                for k, (src, buf) in enumerate(((o_hbm, o_buf), (g_hbm, g_buf), (x_hbm, x_buf)))]

    @pl.when(s == 0)
    def _():
        for t in range(min(n_buf - 1, n_steps)):
            for cp in copies(t, t):
                cp.start()
        w_s[...] = w_ref[...].astype(BF16)

    ahead = s + (n_buf - 1)

    @pl.when(ahead < n_steps)
    def _():
        for cp in copies(ahead, lax.rem(ahead, n_buf)):
            cp.start()

    slot = lax.rem(s, n_buf)
    for cp in copies(s, slot):
        cp.wait()
    gate = cond_ref[:, 2 * d:3 * d]
    y = (o_buf[slot].astype(F32) * _silu(g_buf[slot].astype(F32))).astype(BF16)
    x2 = x_buf[slot] + gate * _dot(y, w_s[...])
    out_ref[...] = _normed(x2) * fgain_ref[...]


def _sb_out(o2, g, x1, cond, fgain, w_out, seq, tm=512, n_buf=4):
    n, d = x1.shape
    assert seq % tm == 0 and n_buf >= 2
    tpb = seq // tm
    n_steps = n // tm
    fixed = lambda i: (0, 0)
    hbm = pl.BlockSpec(memory_space=pl.ANY)
    return pl.pallas_call(
        functools.partial(_sb_out_kernel, d=d, tm=tm, n_steps=n_steps, n_buf=n_buf),
        grid=(n_steps,),
        in_specs=[hbm, hbm, hbm,
                  pl.BlockSpec((None, 1, 3 * d), lambda i: (i // tpb, 0, 0)),
                  pl.BlockSpec((1, d), fixed),
                  pl.BlockSpec((d, d), fixed, pipeline_mode=pl.Buffered(1))],
        out_specs=pl.BlockSpec((tm, d), lambda i: (i, 0)),
        out_shape=jax.ShapeDtypeStruct((n, d), F32),
        scratch_shapes=[pltpu.VMEM((n_buf, tm, d), o2.dtype), pltpu.VMEM((n_buf, tm, d), g.dtype),
                        pltpu.VMEM((n_buf, tm, d), x1.dtype), pltpu.SemaphoreType.DMA((3, n_buf)),
                        pltpu.VMEM((d, d), BF16)],
        compiler_params=_params(1),
        name="sb_out",
    )(o2, g, x1, cond, fgain, w_out)


def kernel(x, c, norm_gain, w_ada, b_ada, gla_w_in, gla_w_gk2, gla_b_gk, gla_o_gain, gla_w_out,
           kv_gain, kv_w_ada, kv_b_ada, w_kv, sb_w_in, sb_w_out, final_gain):
    bsz, seq, d = x.shape
    assert w_ada.shape[0] == 2 and gla_w_in.shape[0] == 1 and sb_w_in.shape[0] == 1
    dk = gla_w_gk2.shape[-1]
    n = bsz * seq
    x2 = x.reshape(n, d)

    cond = _ada(c, w_ada, b_ada)
    cond_kv = _ada(c, kv_w_ada[None], kv_b_ada[None])[0]
    cond1 = cond[0].reshape(bsz, 1, 3 * d)
    cond2 = cond[1].reshape(bsz, 1, 3 * d)
    cond_kv = cond_kv.reshape(bsz, 1, 2 * d)

    w_in = jnp.swapaxes(gla_w_in[0], 0, 1)
    n_main = 2 * dk + 2 * d
    w_lr = jnp.pad(w_in[n_main:], ((0, LANES - GLA_GATE_RANK), (0, 0))).astype(BF16)
    w_gk2 = jnp.pad(gla_w_gk2[0], ((0, LANES - GLA_GATE_RANK), (0, 0))).astype(BF16)
    qe, ke, v, g, dec = _gla_proj(x2, cond1, norm_gain[0:1], w_in, w_lr, w_gk2,
                                  gla_b_gk[0:1], seq)
    x1 = _gla_core(qe, ke, v, dec, g, x2, cond1, gla_o_gain[0:1], gla_w_out[0], bsz, seq)

    dh = d // SB_HEADS
    qscale = (dh ** -0.5) * LOG2E
    ks, vs, qs, g2 = _kvq_proj(x1, cond_kv, cond2, kv_gain.reshape(1, d), norm_gain[1:2],
                               w_kv, sb_w_in[0], seq, qscale)
    o2 = _sb_attn(qs, ks, vs, bsz, seq)
    out = _sb_out(o2.reshape(n, d), g2, x1, cond2, final_gain.reshape(1, d),
                  sb_w_out[0], seq)
    return out.reshape(bsz, seq, d)
```

```python
import functools

import jax
import jax.numpy as jnp
from jax import lax
from jax.experimental import pallas as pl
from jax.experimental.pallas import tpu as pltpu

F32 = jnp.float32
BF16 = jnp.bfloat16

RMS_EPS = 1e-6
GLA_HEADS = 4
GLA_CHUNK = 64
GLA_GATE_RANK = 16
GLA_GATE_NORMALIZER = 16.0
SB_HEADS = 16
LANES = 128
MXU_TILE = 256
LOG2E = 1.4426950408889634
VMEM_LIMIT = 48 * 1024 * 1024
VMEM_LIMIT_LARGE = 56 * 1024 * 1024
SB_SKIP_LOG2 = 160.0
SB_MASKED_Z = -1e30
SB_EXP2_CLAMP = 126.0


def _dot(a, b):
    return jnp.dot(a, b, preferred_element_type=F32)


def _dot_nt(a, b):
    return lax.dot_general(a, b, (((1,), (1,)), ((), ())), preferred_element_type=F32)


def _silu(x):
    return x * jax.nn.sigmoid(x)


def _normed(x):
    return x * lax.rsqrt(jnp.mean(x * x, axis=-1, keepdims=True) + RMS_EPS)


def _split_bf16(x):
    hi = x.astype(BF16)
    return hi, (x - hi.astype(F32)).astype(BF16)


def _params(n_axes, vmem_limit=VMEM_LIMIT):
    return pltpu.CompilerParams(dimension_semantics=("arbitrary",) * n_axes,
                                vmem_limit_bytes=vmem_limit)


def _ada_kernel(c_ref, w_ref, b_ref, o_ref):
    hi, lo = _split_bf16(_silu(c_ref[...]))
    w = w_ref[...].astype(BF16)
    o_ref[...] = _dot(hi, w) + _dot(lo, w) + b_ref[...]


def _ada(c, w, b, tn=1024):
    n_l, d, n = w.shape
    bsz = c.shape[0]
    return pl.pallas_call(
        _ada_kernel,
        grid=(n_l, n // tn),
        in_specs=[pl.BlockSpec((bsz, d), lambda l, j: (0, 0)),
                  pl.BlockSpec((None, d, tn), lambda l, j: (l, 0, j)),
                  pl.BlockSpec((None, 1, tn), lambda l, j: (l, 0, j))],
        out_specs=pl.BlockSpec((None, bsz, tn), lambda l, j: (l, 0, j)),
        out_shape=jax.ShapeDtypeStruct((n_l, bsz, n), F32),
        compiler_params=_params(2),
        name="ada",
    )(c, w, b.reshape(n_l, 1, n))


def _gla_proj_kernel(x_ref, cond_ref, gain_ref, w_ref, wlr_ref, wgk_ref, bgk_ref,
                     qe_ref, ke_ref, v_ref, g_ref, dec_ref, w_s, *, d, dk, tm, qscale):
    n_main = 2 * dk + 2 * d

    @pl.when(pl.program_id(0) == 0)
    def _():
        w_s[...] = w_ref[0:n_main, :].astype(BF16)

    c = GLA_CHUNK
    half = MXU_TILE
    sub = 2 * half
    shift = cond_ref[:, 0:d]
    scale = cond_ref[:, d:2 * d]
    ri = lax.broadcasted_iota(jnp.int32, (half, half), 0)
    ci = lax.broadcasted_iota(jnp.int32, (half, half), 1)
    same_chunk = (ri // c) == (ci // c)
    tril = jnp.where(jnp.logical_and(same_chunk, ci <= ri), 1.0, 0.0).astype(BF16)
    tril2 = jnp.concatenate([tril, tril], axis=1)
    n_c = sub // c
    cj = lax.broadcasted_iota(jnp.int32, (n_c, sub), 1) // c
    cr = lax.broadcasted_iota(jnp.int32, (n_c, sub), 0)
    sel = jnp.where(cj == cr, 1.0, 0.0).astype(BF16)

    for r in range(tm // sub):
        r0 = r * sub
        h = (_normed(x_ref[r0:r0 + sub, :]) * gain_ref[...] * (1.0 + scale) + shift).astype(BF16)
        lr = jnp.concatenate([_dot_nt(h[:half], wlr_ref[...]), _dot_nt(h[half:], wlr_ref[...])],
                             axis=0).astype(BF16)
        pre = _dot(lr, wgk_ref[...]) + bgk_ref[...]
        q = _dot_nt(h, w_s[0:dk, :])
        k = _dot_nt(h, w_s[dk:2 * dk, :])
        gk = jax.nn.log_sigmoid(pre) * (1.0 / GLA_GATE_NORMALIZER)
        hi, lo = _split_bf16(gk)
        b_last = _dot(jnp.concatenate([sel, sel], axis=1), jnp.concatenate([hi, lo], axis=0))
        dec_ref[r * n_c:(r + 1) * n_c, :] = jnp.exp(b_last)
        bs = [_dot(tril2, jnp.concatenate([hi[s * half:(s + 1) * half], lo[s * half:(s + 1) * half]], axis=0))
              for s in range(sub // half)]
        v_ref[r0:r0 + sub, :] = _dot_nt(h, w_s[2 * dk:2 * dk + d, :]).astype(BF16)
        g_ref[r0:r0 + sub, :] = _dot_nt(h, w_s[2 * dk + d:2 * dk + 2 * d, :]).astype(BF16)
        for s, b in enumerate(bs):
            rows = slice(s * half, (s + 1) * half)
            out_rows = slice(r0 + s * half, r0 + (s + 1) * half)
            qe_ref[out_rows, :] = (q[rows] * qscale * jnp.exp(b)).astype(BF16)
            ke_ref[out_rows, :] = (k[rows] * jnp.exp(-b)).astype(BF16)


def _gla_proj(x2, cond, gain, w_in, w_lr, w_gk2, b_gk, seq, tm=1024):
    n, d = x2.shape
    dk = w_gk2.shape[1]
    assert seq % tm == 0 and tm % (2 * MXU_TILE) == 0 and MXU_TILE % GLA_CHUNK == 0
    tpb = seq // tm
    n_c = tm // GLA_CHUNK
    row = lambda i: (i, 0)
    fixed = lambda i: (0, 0)
    qscale = (dk // GLA_HEADS) ** -0.5
    return pl.pallas_call(
        functools.partial(_gla_proj_kernel, d=d, dk=dk, tm=tm, qscale=qscale),
        grid=(n // tm,),
        in_specs=[pl.BlockSpec((tm, d), row),
                  pl.BlockSpec((None, 1, 3 * d), lambda i: (i // tpb, 0, 0)),
                  pl.BlockSpec((1, d), fixed),
                  pl.BlockSpec(w_in.shape, fixed, pipeline_mode=pl.Buffered(1)),
                  pl.BlockSpec(w_lr.shape, fixed),
                  pl.BlockSpec(w_gk2.shape, fixed),
                  pl.BlockSpec((1, dk), fixed)],
        out_specs=[pl.BlockSpec((tm, dk), row), pl.BlockSpec((tm, dk), row),
                   pl.BlockSpec((tm, d), row), pl.BlockSpec((tm, d), row),
                   pl.BlockSpec((n_c, dk), row)],
        out_shape=[jax.ShapeDtypeStruct((n, dk), BF16), jax.ShapeDtypeStruct((n, dk), BF16),
                   jax.ShapeDtypeStruct((n, d), BF16), jax.ShapeDtypeStruct((n, d), BF16),
                   jax.ShapeDtypeStruct((n // GLA_CHUNK, dk), F32)],
        scratch_shapes=[pltpu.VMEM((2 * dk + 2 * d, d), BF16)],
        compiler_params=_params(1),
        name="gla_proj",
    )(x2, cond, gain, w_in, w_lr, w_gk2, b_gk)


def _gla_core_kernel(qe_ref, ke_ref, v_ref, dec_ref, g_ref, x_ref, cond_ref, ogain_ref, w_ref,
                     x1_ref, s_ref, w_s, y_s, *, tile, tiles_per_seq, d, hk, hv):
    c = GLA_CHUNK
    n_c = tile // c
    heads = range(GLA_HEADS)
    ri = lax.broadcasted_iota(jnp.int32, (c, c), 0)
    ci = lax.broadcasted_iota(jnp.int32, (c, c), 1)
    causal = ci <= ri

    @pl.when(pl.program_id(0) == 0)
    def _():
        w_s[...] = w_ref[...].astype(BF16)

    @pl.when(pl.program_id(0) % tiles_per_seq == 0)
    def _():
        s_ref[...] = jnp.zeros(s_ref.shape, F32)

    def chunk(ref, h, u, width):
        return ref[u * c:(u + 1) * c, h * width:(h + 1) * width]

    work = [(h, u) for u in range(n_c) for h in heads]
    qs = {hu: chunk(qe_ref, *hu, hk) for hu in work}
    ks = {hu: chunk(ke_ref, *hu, hk) for hu in work}
    vs = {hu: chunk(v_ref, *hu, hv) for hu in work}
    scores = {hu: _dot_nt(qs[hu], ks[hu]) for hu in work}
    masked = {hu: jnp.where(causal, scores[hu], 0.0).astype(BF16) for hu in work}
    dec_t = [dec_ref[:, h * hk:(h + 1) * hk].T for h in heads]
    s = [s_ref[h] for h in heads]
    for u in range(n_c):
        rows = slice(u * c, (u + 1) * c)
        incs = [_dot(ks[h, u].T, vs[h, u]) for h in heads]
        o = []
        for h in heads:
            lhs = jnp.concatenate([qs[h, u], masked[h, u]], axis=1)
            rhs = jnp.concatenate([s[h].astype(BF16), vs[h, u]], axis=0)
            o.append(_dot(lhs, rhs))
            s[h] = (s[h] + incs[h]) * dec_t[h][:, u:u + 1]
        y = jnp.concatenate([_normed(o[h]) * ogain_ref[...] for h in heads], axis=1)
        y_s[rows, :] = (y * _silu(g_ref[rows, :].astype(F32))).astype(BF16)
    for h in heads:
        s_ref[h] = s[h]

    x1_ref[...] = x_ref[...] + cond_ref[:, 2 * d:3 * d] * _dot(y_s[...], w_s[...])


def _gla_core(qe, ke, v, dec, g, x2, cond, ogain, w_out, bsz, seq, tile=1024):
    d = x2.shape[-1]
    dk = qe.shape[-1]
    assert seq % tile == 0 and tile % GLA_CHUNK == 0 and (tile // GLA_CHUNK) % 8 == 0
    n_c = tile // GLA_CHUNK
    tpb = seq // tile
    row = lambda i: (i, 0)
    fixed = lambda i: (0, 0)
    return pl.pallas_call(
        functools.partial(_gla_core_kernel, tile=tile, tiles_per_seq=tpb, d=d, hk=dk // GLA_HEADS,
                          hv=d // GLA_HEADS),
        grid=(bsz * tpb,),
        in_specs=[pl.BlockSpec((tile, dk), row), pl.BlockSpec((tile, dk), row),
                  pl.BlockSpec((tile, d), row), pl.BlockSpec((n_c, dk), row),
                  pl.BlockSpec((tile, d), row), pl.BlockSpec((tile, d), row),
                  pl.BlockSpec((None, 1, 3 * d), lambda i: (i // tpb, 0, 0)),
                  pl.BlockSpec((1, d // GLA_HEADS), fixed),
                  pl.BlockSpec((d, d), fixed, pipeline_mode=pl.Buffered(1))],
        out_specs=pl.BlockSpec((tile, d), row),
        out_shape=jax.ShapeDtypeStruct((bsz * seq, d), F32),
        scratch_shapes=[pltpu.VMEM((GLA_HEADS, dk // GLA_HEADS, d // GLA_HEADS), F32),
                        pltpu.VMEM((d, d), BF16), pltpu.VMEM((tile, d), BF16)],
        compiler_params=_params(1),
        name="gla_core",
    )(qe, ke, v, dec, g, x2, cond, ogain, w_out)


def _kvq_proj_kernel(x_ref, ckv_ref, c2_ref, kvgain_ref, gain_ref, wkv_ref, win_ref,
                     k_ref, v_ref, q_ref, g_ref, wkv_s, win_s, *, d, qscale, sub):
    @pl.when(pl.program_id(0) == 0)
    def _():
        wkv_s[...] = wkv_ref[...].astype(BF16)
        win_s[...] = win_ref[...].astype(BF16)

    for r in range(x_ref.shape[0] // sub):
        rows = slice(r * sub, (r + 1) * sub)
        y = _normed(x_ref[rows, :])
        hkv = (y * kvgain_ref[...] * (1.0 + ckv_ref[:, d:2 * d]) + ckv_ref[:, 0:d]).astype(BF16)
        h2 = (y * gain_ref[...] * (1.0 + c2_ref[:, d:2 * d]) + c2_ref[:, 0:d]).astype(BF16)
        k_ref[rows, :] = _dot(hkv, wkv_s[:, 0:d]).astype(BF16)
        v_ref[rows, :] = _dot(hkv, wkv_s[:, d:2 * d]).astype(BF16)
        q_ref[rows, :] = (_dot(h2, win_s[:, 0:d]) * qscale).astype(BF16)
        g_ref[rows, :] = _dot(h2, win_s[:, d:2 * d]).astype(BF16)


def _kvq_proj(x1, ckv, c2, kvgain, gain, w_kv, w_in, seq, qscale, tm=1024, sub=512):
    n, d = x1.shape
    assert seq % tm == 0 and tm % sub == 0
    tpb = seq // tm
    row = lambda i: (i, 0)
    fixed = lambda i: (0, 0)
    once = dict(pipeline_mode=pl.Buffered(1))
    return pl.pallas_call(
        functools.partial(_kvq_proj_kernel, d=d, qscale=qscale, sub=sub),
        grid=(n // tm,),
        in_specs=[pl.BlockSpec((tm, d), row),
                  pl.BlockSpec((None, 1, 2 * d), lambda i: (i // tpb, 0, 0)),
                  pl.BlockSpec((None, 1, 3 * d), lambda i: (i // tpb, 0, 0)),
                  pl.BlockSpec((1, d), fixed), pl.BlockSpec((1, d), fixed),
                  pl.BlockSpec((d, 2 * d), fixed, **once), pl.BlockSpec((d, 2 * d), fixed, **once)],
        out_specs=[pl.BlockSpec((tm, d), row)] * 4,
        out_shape=[jax.ShapeDtypeStruct((n, d), BF16)] * 4,
        scratch_shapes=[pltpu.VMEM((d, 2 * d), BF16), pltpu.VMEM((d, 2 * d), BF16)],
        compiler_params=_params(1, VMEM_LIMIT_LARGE),
        name="kvq_proj",
    )(x1, ckv, c2, kvgain, gain, w_kv, w_in)


def _sb_attn_kernel(q_ref, k_ref, v_ref, o_ref, cmin_s, *, seq, blk, dh, pairs, fused):
    n_blk = seq // blk
    pw = 2 * dh
    lane = lax.broadcasted_iota(jnp.int32, (blk, pw), 1)
    ri = lax.broadcasted_iota(jnp.int32, (blk, blk), 0)
    ci = lax.broadcasted_iota(jnp.int32, (blk, blk), 1)
    strict = jnp.where(ci < ri, 1.0, 0.0).astype(BF16)
    ri2 = lax.broadcasted_iota(jnp.int32, (2 * blk, blk), 0)
    ci2 = lax.broadcasted_iota(jnp.int32, (2 * blk, blk), 1)
    below = ci2 < (ri2 & (blk - 1))

    def rows(j):
        return pl.ds(pl.multiple_of(j * blk, blk), blk)

    def lanes(p):
        return slice(p * pw, (p + 1) * pw)

    def stacked_q(i, p):
        q2 = q_ref[rows(i), lanes(p)]
        zero = jnp.zeros_like(q2)
        return jnp.concatenate([jnp.where(lane < dh, q2, zero), jnp.where(lane >= dh, q2, zero)], axis=0)

    def scores(qp, j, p):
        return _dot_nt(qp, k_ref[rows(j), lanes(p)])

    def soft(z, diagonal):
        if diagonal:
            z = jnp.where(below, z, SB_MASKED_Z)
        sp = jnp.maximum(jnp.log2(1.0 + jnp.exp2(jnp.minimum(z, SB_EXP2_CLAMP))), z)
        return z - sp, sp

    def weights(lb, sp, sums, carry):
        tot = sums if carry is None else sums + jnp.concatenate([carry] * (blk // LANES), axis=1)
        w = jnp.exp2(lb - tot).astype(BF16)
        return w, jnp.broadcast_to(tot[:, 0:1] + sp[:, 0:1], (2 * blk, LANES))

    def finish(i, p, acc):
        o_ref[rows(i), lanes(p)] = jnp.where(lane < dh, acc[:blk], acc[blk:]).astype(o_ref.dtype)

    def qk_stage(i):
        qs = [stacked_q(i, p) for p in range(pairs)]
        return {(p, n): scores(qs[p], i - n, p) for n in range(min(i + 1, fused)) for p in range(pairs)}

    zs = qk_stage(0)
    lowest = None
    for i in range(n_blk):
        work = sorted(zs, key=lambda pn: (pn[1], pn[0]))
        soft_i = {key: soft(zs[key], key[1] == 0) for key in work}
        sums = {key: _dot(soft_i[key][1].astype(BF16), strict) for key in work}
        if i + 1 < n_blk:
            zs = qk_stage(i + 1)
        carries = [None] * pairs
        ws = {}
        for p, n in work:
            ws[p, n], carries[p] = weights(*soft_i[p, n], sums[p, n], carries[p])
        for p in range(pairs):
            acc = _dot(ws[p, 0], v_ref[rows(i), lanes(p)])
            for n in range(1, min(i + 1, fused)):
                acc = acc + _dot(ws[p, n], v_ref[rows(i - n), lanes(p)])
            finish(i, p, acc)
            cmin = jnp.min(carries[p])
            cmin_s[i * pairs + p] = cmin
            if i >= fused:
                lowest = cmin if lowest is None else jnp.minimum(lowest, cmin)

    def fix(i, _):
        for p in range(pairs):
            @pl.when(cmin_s[i * pairs + p] <= SB_SKIP_LOG2)
            def _():
                qp = stacked_q(i, p)

                def cond(st):
                    j, cmin, _, _ = st
                    return jnp.logical_and(j >= 0, cmin <= SB_SKIP_LOG2)

                def body(st):
                    j, _, carry, acc = st
                    lb, sp = soft(scores(qp, j, p), False)
                    w, carry = weights(lb, sp, _dot(sp.astype(BF16), strict), carry)
                    return j - 1, jnp.min(carry), carry, acc + _dot(w, v_ref[rows(j), lanes(p)])

                lb, sp = soft(scores(qp, i, p), True)
                w, carry = weights(lb, sp, _dot(sp.astype(BF16), strict), None)
                st = lax.while_loop(cond, body, (i - 1, jnp.min(carry), carry,
                                                 _dot(w, v_ref[rows(i), lanes(p)])))
                finish(i, p, st[3])
        return 0

    if lowest is not None:
        @pl.when(lowest <= SB_SKIP_LOG2)
        def _():
            lax.fori_loop(fused, n_blk, fix, 0)


def _sb_attn(q, k, v, bsz, seq, blk=128, pairs=1, fused=3):
    d = q.shape[-1]
    dh = d // SB_HEADS
    gw = pairs * 2 * dh
    assert 2 * dh == LANES and seq % blk == 0 and d % gw == 0
    q3, k3, v3 = (t.reshape(bsz, seq, d) for t in (q, k, v))
    spec = pl.BlockSpec((None, seq, gw), lambda b, p: (b, 0, p))
    return pl.pallas_call(
        functools.partial(_sb_attn_kernel, seq=seq, blk=blk, dh=dh, pairs=pairs, fused=fused),
        grid=(bsz, d // gw),
        in_specs=[spec, spec, spec],
        out_specs=spec,
        out_shape=jax.ShapeDtypeStruct((bsz, seq, d), BF16),
        scratch_shapes=[pltpu.SMEM((seq // blk * pairs,), F32)],
        compiler_params=_params(2),
        name="sb_attn",
    )(q3, k3, v3)


def _sb_out_kernel(o_hbm, g_hbm, x_hbm, cond_ref, fgain_ref, w_ref, out_ref,
                   o_buf, g_buf, x_buf, sems, w_s, *, d, tm, n_steps, n_buf, sub):
    s = pl.program_id(0)

    def copies(tile, slot):
        rows = pl.ds(pl.multiple_of(tile * tm, tm), tm)
        return [pltpu.make_async_copy(src.at[rows, :], buf.at[slot], sems.at[k, slot])
                for k, (src, buf) in enumerate(((o_hbm, o_buf), (g_hbm, g_buf), (x_hbm, x_buf)))]

    @pl.when(s == 0)
    def _():
        for t in range(min(n_buf - 1, n_steps)):
            for cp in copies(t, t):
                cp.start()
        w_s[...] = w_ref[...].astype(BF16)

    ahead = s + (n_buf - 1)

    @pl.when(ahead < n_steps)
    def _():
        for cp in copies(ahead, lax.rem(ahead, n_buf)):
            cp.start()

    slot = lax.rem(s, n_buf)
    for cp in copies(s, slot):
        cp.wait()
    gate = cond_ref[:, 2 * d:3 * d]
    for h in range(tm // sub):
        r = slice(h * sub, (h + 1) * sub)
        y = (o_buf.at[slot][r, :].astype(F32) * _silu(g_buf.at[slot][r, :].astype(F32))).astype(BF16)
        x2 = x_buf.at[slot][r, :] + gate * _dot(y, w_s[...])
        out_ref[r, :] = _normed(x2) * fgain_ref[...]


def _sb_out(o2, g, x1, cond, fgain, w_out, seq, tm=1024, n_buf=3, sub=256):
    n, d = x1.shape
    assert seq % tm == 0 and n_buf >= 2 and tm % sub == 0
    tpb = seq // tm
    n_steps = n // tm
    fixed = lambda i: (0, 0)
    hbm = pl.BlockSpec(memory_space=pl.ANY)
    return pl.pallas_call(
        functools.partial(_sb_out_kernel, d=d, tm=tm, n_steps=n_steps, n_buf=n_buf, sub=sub),
        grid=(n_steps,),
        in_specs=[hbm, hbm, hbm,
                  pl.BlockSpec((None, 1, 3 * d), lambda i: (i // tpb, 0, 0)),
                  pl.BlockSpec((1, d), fixed),
                  pl.BlockSpec((d, d), fixed, pipeline_mode=pl.Buffered(1))],
        out_specs=pl.BlockSpec((tm, d), lambda i: (i, 0)),
        out_shape=jax.ShapeDtypeStruct((n, d), F32),
        scratch_shapes=[pltpu.VMEM((n_buf, tm, d), o2.dtype), pltpu.VMEM((n_buf, tm, d), g.dtype),
                        pltpu.VMEM((n_buf, tm, d), x1.dtype), pltpu.SemaphoreType.DMA((3, n_buf)),
                        pltpu.VMEM((d, d), BF16)],
        compiler_params=_params(1),
        name="sb_out",
    )(o2, g, x1, cond, fgain, w_out)


def kernel(x, c, norm_gain, w_ada, b_ada, gla_w_in, gla_w_gk2, gla_b_gk, gla_o_gain, gla_w_out,
           kv_gain, kv_w_ada, kv_b_ada, w_kv, sb_w_in, sb_w_out, final_gain):
    bsz, seq, d = x.shape
    assert w_ada.shape[0] == 2 and gla_w_in.shape[0] == 1 and sb_w_in.shape[0] == 1
    dk = gla_w_gk2.shape[-1]
    n = bsz * seq
    x2 = x.reshape(n, d)

    cond = _ada(c, w_ada, b_ada)
    cond_kv = _ada(c, kv_w_ada[None], kv_b_ada[None])[0]
    cond1 = cond[0].reshape(bsz, 1, 3 * d)
    cond2 = cond[1].reshape(bsz, 1, 3 * d)
    cond_kv = cond_kv.reshape(bsz, 1, 2 * d)

    w_in = jnp.swapaxes(gla_w_in[0], 0, 1)
    n_main = 2 * dk + 2 * d
    w_lr = jnp.pad(w_in[n_main:], ((0, LANES - GLA_GATE_RANK), (0, 0))).astype(BF16)
    w_gk2 = jnp.pad(gla_w_gk2[0], ((0, LANES - GLA_GATE_RANK), (0, 0))).astype(BF16)
    qe, ke, v, g, dec = _gla_proj(x2, cond1, norm_gain[0:1], w_in, w_lr, w_gk2,
                                  gla_b_gk[0:1], seq)
    x1 = _gla_core(qe, ke, v, dec, g, x2, cond1, gla_o_gain[0:1], gla_w_out[0], bsz, seq)

    dh = d // SB_HEADS
    qscale = (dh ** -0.5) * LOG2E
    ks, vs, qs, g2 = _kvq_proj(x1, cond_kv, cond2, kv_gain.reshape(1, d), norm_gain[1:2],
                               w_kv, sb_w_in[0], seq, qscale)
    o2 = _sb_attn(qs, ks, vs, bsz, seq)
    out = _sb_out(o2.reshape(n, d), g2, x1, cond2, final_gain.reshape(1, d),
                  sb_w_out[0], seq)
    return out.reshape(bsz, seq, d)
```
